```python
import math
import jax, jax.numpy as jnp
from jax import lax
import numpy as np

D_MODEL = 1024
BATCH = 8
SEQ = 8192
DEPTH = 4
DEC_BATCH = 8
DEC_SEQ = 4096
PAST_LEN = 128

MIX_WIDTH = D_MODEL
ATTN_WIDTH = MIX_WIDTH // 2
GMLP_WIDTH = MIX_WIDTH - ATTN_WIDTH
ATT_HEAD_DIM = 64
N_ATT_HEADS = ATTN_WIDTH // (2 * ATT_HEAD_DIM)
V_HEAD_DIM = 2 * ATT_HEAD_DIM
ROT_DIM = ATT_HEAD_DIM // 4
ROPE_THETA = 500000.0
Q_BLOCK = 128
CHUNK = 128
N_GMLP_GROUPS = 4
GMLP_GROUP_DIM = GMLP_WIDTH // N_GMLP_GROUPS
D_FF = 3584
N_EXPERTS = 8
TOP_K = 2
PLE_DIM = 256
N_DENSE = (DEPTH + 1) // 2
N_MOE = DEPTH // 2
IN_PROJ = 3 * ATTN_WIDTH + 2 * GMLP_WIDTH
EPS = 1e-6

kernel_name = 'hybrid_diffattn_gmlp_encoder'


def rmsnorm(x, g):
    xf = x.astype(jnp.float32)
    y = xf * lax.rsqrt(jnp.mean(jnp.square(xf), axis=-1, keepdims=True) + EPS)
    return (y * g.astype(jnp.float32)).astype(x.dtype)


def rmsnorm_plain(x):
    xf = x.astype(jnp.float32)
    return (xf * lax.rsqrt(jnp.mean(jnp.square(xf), axis=-1, keepdims=True) + EPS)).astype(x.dtype)


def rope_partial(x, pos):
    half = ROT_DIM // 2
    inv = ROPE_THETA ** (-jnp.arange(half, dtype=jnp.float32) * 2.0 / ROT_DIM)
    ang = pos[:, None] * inv[None, :]
    cos = jnp.cos(ang)[None, :, None, None, :]
    sin = jnp.sin(ang)[None, :, None, None, :]
    xr = x[..., :ROT_DIM].astype(jnp.float32)
    x1, x2 = xr[..., :half], xr[..., half:]
    rot = jnp.concatenate([x1 * cos - x2 * sin, x2 * cos + x1 * sin], axis=-1).astype(x.dtype)
    return jnp.concatenate([rot, x[..., ROT_DIM:]], axis=-1)


def diff_attention(q, k, v, lam):
    B, S = q.shape[0], q.shape[1]
    nb = S // Q_BLOCK
    scale = 1.0 / math.sqrt(ATT_HEAD_DIM)
    qb = q.reshape(B, nb, Q_BLOCK, N_ATT_HEADS, 2, ATT_HEAD_DIM).transpose(1, 0, 2, 3, 4, 5)
    vf = v.astype(jnp.float32)

    def block(qi):
        s = jnp.einsum('bqhcd,bkhcd->bhcqk', qi, k, preferred_element_type=jnp.float32) * scale
        pr = jax.nn.softmax(s, axis=-1)
        a = pr[:, :, 0] - lam * pr[:, :, 1]
        return jnp.einsum('bhqk,bkhe->bqhe', a, vf)

    o = lax.map(block, qb)
    return o.transpose(1, 0, 2, 3, 4).reshape(B, S, N_ATT_HEADS, V_HEAD_DIM).astype(v.dtype)


def gmlp_mix(u, vg, ln_g, ln_b, ws, bs):
    B, S = u.shape[0], u.shape[1]
    u = jax.nn.gelu(u)
    vf = jax.nn.gelu(vg.astype(jnp.float32))
    mu = jnp.mean(vf, axis=-1, keepdims=True)
    var = jnp.mean(jnp.square(vf - mu), axis=-1, keepdims=True)
    vn = ((vf - mu) * lax.rsqrt(var + EPS) * ln_g.astype(jnp.float32) + ln_b.astype(jnp.float32)).astype(u.dtype)
    vc = vn.reshape(B, S // CHUNK, CHUNK, N_GMLP_GROUPS, GMLP_GROUP_DIM)
    mixed = jnp.einsum('gij,bcjgd->bcigd', ws, vc) + bs.T[:, :, None]
    return u * mixed.reshape(B, S, GMLP_WIDTH)


def swiglu(h, wg, wu, wd):
    return (jax.nn.silu(h @ wg) * (h @ wu)) @ wd


def moe_ffn(h, router, wg, wu, wd):
    B, S, D = h.shape
    t = h.reshape(-1, D)
    logits = jnp.matmul(t, router, preferred_element_type=jnp.float32)
    top_v, top_i = lax.top_k(logits, TOP_K)
    top_w = jax.nn.softmax(top_v, axis=-1)
    gates = jnp.sum(jax.nn.one_hot(top_i, N_EXPERTS, dtype=jnp.float32) * top_w[..., None], axis=1)
    out = jnp.zeros((t.shape[0], D), jnp.float32)
    for e in range(N_EXPERTS):
        out = out + gates[:, e:e + 1] * swiglu(t, wg[e], wu[e], wd[e]).astype(jnp.float32)
    return out.astype(h.dtype).reshape(B, S, D)


def trunk(x, p, norm_mix, w_in, lambda_q1, lambda_k1, lambda_q2, lambda_k2, subln,
          gmlp_ln_g, gmlp_ln_b, gmlp_ws, gmlp_bs, w_out, norm_ffn,
          dense_w_gate, dense_w_up, dense_w_down, router, moe_w_gate, moe_w_up, moe_w_down,
          ple_w, ple_norm, ple_gate_w, ple_gate_b, final_norm):
    B, S, _ = x.shape
    pos = jnp.arange(S, dtype=jnp.float32)
    a0, a1, a2, a3 = ATTN_WIDTH, 2 * ATTN_WIDTH, 3 * ATTN_WIDTH, 3 * ATTN_WIDTH + GMLP_WIDTH
    for i in range(DEPTH):
        h = rmsnorm(x, norm_mix[i])
        proj = h @ w_in[i]
        q = proj[..., :a0].reshape(B, S, N_ATT_HEADS, 2, ATT_HEAD_DIM)
        k = proj[..., a0:a1].reshape(B, S, N_ATT_HEADS, 2, ATT_HEAD_DIM)
        v = proj[..., a1:a2].reshape(B, S, N_ATT_HEADS, V_HEAD_DIM)
        u = proj[..., a2:a3]
        vg = proj[..., a3:]
        q = rope_partial(q, pos)
        k = rope_partial(k, pos)
        lam_init = 0.8 - 0.6 * math.exp(-0.3 * i)
        lam = (jnp.exp(jnp.sum(lambda_q1[i].astype(jnp.float32) * lambda_k1[i].astype(jnp.float32)))
               - jnp.exp(jnp.sum(lambda_q2[i].astype(jnp.float32) * lambda_k2[i].astype(jnp.float32)))
               + lam_init)
        att = diff_attention(q, k, v, lam)
        att = (rmsnorm(att, subln[i]) * (1.0 - lam_init)).reshape(B, S, ATTN_WIDTH)
        gm = gmlp_mix(u, vg, gmlp_ln_g[i], gmlp_ln_b[i], gmlp_ws[i], gmlp_bs[i])
        x = x + jnp.concatenate([att, gm], axis=-1) @ w_out[i]
        h = rmsnorm(x, norm_ffn[i])
        if i % 2 == 0:
            j = i // 2
            f = swiglu(h, dense_w_gate[j], dense_w_up[j], dense_w_down[j])
        else:
            j = i // 2
            f = moe_ffn(h, router[j], moe_w_gate[j], moe_w_up[j], moe_w_down[j])
        x = x + f
        gate = jax.nn.sigmoid(rmsnorm_plain(x) @ ple_gate_w[i] + ple_gate_b[i])
        e = rmsnorm(p[i] @ ple_w[i], ple_norm[i])
        x = x + gate * e
    return rmsnorm(x, final_norm)


def setup_inputs(seed: int = 0) -> dict:
    key = jax.random.key(seed)
    ks = jax.random.split(key, 32)
    f32 = jnp.float32

    def nrm(k, shape, scale):
        return jax.random.normal(k, shape, f32) * scale

    def gain(k, shape):
        return 1.0 + 0.1 * jax.random.normal(k, shape, f32)

    return {
        'x_prompt': jax.random.normal(ks[0], (BATCH, SEQ, D_MODEL), f32),
        'x_sample': jax.random.normal(ks[1], (DEC_BATCH, DEC_SEQ, D_MODEL), f32),
        'p_prompt': jax.random.normal(ks[2], (DEPTH, BATCH, SEQ, PLE_DIM), f32),
        'p_sample': jax.random.normal(ks[3], (DEPTH, DEC_BATCH, DEC_SEQ, PLE_DIM), f32),
        'norm_mix': gain(ks[4], (DEPTH, D_MODEL)),
        'w_in': nrm(ks[5], (DEPTH, D_MODEL, IN_PROJ), D_MODEL ** -0.5),
        'lambda_q1': nrm(ks[6], (DEPTH, ATT_HEAD_DIM), 0.1),
        'lambda_k1': nrm(ks[7], (DEPTH, ATT_HEAD_DIM), 0.1),
        'lambda_q2': nrm(ks[8], (DEPTH, ATT_HEAD_DIM), 0.1),
        'lambda_k2': nrm(ks[9], (DEPTH, ATT_HEAD_DIM), 0.1),
        'subln': gain(ks[10], (DEPTH, V_HEAD_DIM)),
        'gmlp_ln_g': gain(ks[11], (DEPTH, GMLP_WIDTH)),
        'gmlp_ln_b': nrm(ks[12], (DEPTH, GMLP_WIDTH), 0.1),
        'gmlp_ws': nrm(ks[13], (DEPTH, N_GMLP_GROUPS, CHUNK, CHUNK), CHUNK ** -0.5),
        'gmlp_bs': gain(ks[14], (DEPTH, N_GMLP_GROUPS, CHUNK)),
        'w_out': nrm(ks[15], (DEPTH, MIX_WIDTH, D_MODEL), MIX_WIDTH ** -0.5),
        'norm_ffn': gain(ks[16], (DEPTH, D_MODEL)),
        'dense_w_gate': nrm(ks[17], (N_DENSE, D_MODEL, D_FF), D_MODEL ** -0.5),
        'dense_w_up': nrm(ks[18], (N_DENSE, D_MODEL, D_FF), D_MODEL ** -0.5),
        'dense_w_down': nrm(ks[19], (N_DENSE, D_FF, D_MODEL), D_FF ** -0.5),
        'router': nrm(ks[20], (N_MOE, D_MODEL, N_EXPERTS), D_MODEL ** -0.5),
        'moe_w_gate': nrm(ks[21], (N_MOE, N_EXPERTS, D_MODEL, D_FF), D_MODEL ** -0.5),
        'moe_w_up': nrm(ks[22], (N_MOE, N_EXPERTS, D_MODEL, D_FF), D_MODEL ** -0.5),
        'moe_w_down': nrm(ks[23], (N_MOE, N_EXPERTS, D_FF, D_MODEL), D_FF ** -0.5),
        'ple_w': nrm(ks[24], (DEPTH, PLE_DIM, D_MODEL), PLE_DIM ** -0.5),
        'ple_norm': gain(ks[25], (DEPTH, D_MODEL)),
        'ple_gate_w': nrm(ks[26], (DEPTH, D_MODEL, D_MODEL), D_MODEL ** -0.5),
        'ple_gate_b': nrm(ks[27], (DEPTH, D_MODEL), 0.1),
        'final_norm': gain(ks[28], (D_MODEL,)),
    }


def reference(x_prompt, x_sample, p_prompt, p_sample, norm_mix, w_in, lambda_q1, lambda_k1,
              lambda_q2, lambda_k2, subln, gmlp_ln_g, gmlp_ln_b, gmlp_ws, gmlp_bs, w_out,
              norm_ffn, dense_w_gate, dense_w_up, dense_w_down, router, moe_w_gate, moe_w_up,
              moe_w_down, ple_w, ple_norm, ple_gate_w, ple_gate_b, final_norm):
    y_prompt = trunk(x_prompt, p_prompt, norm_mix, w_in, lambda_q1, lambda_k1, lambda_q2, lambda_k2,
                     subln, gmlp_ln_g, gmlp_ln_b, gmlp_ws, gmlp_bs, w_out, norm_ffn,
                     dense_w_gate, dense_w_up, dense_w_down, router, moe_w_gate, moe_w_up, moe_w_down,
                     ple_w, ple_norm, ple_gate_w, ple_gate_b, final_norm)
    y_sample = trunk(x_sample, p_sample, norm_mix, w_in, lambda_q1, lambda_k1, lambda_q2, lambda_k2,
                     subln, gmlp_ln_g, gmlp_ln_b, gmlp_ws, gmlp_bs, w_out, norm_ffn,
                     dense_w_gate, dense_w_up, dense_w_down, router, moe_w_gate, moe_w_up, moe_w_down,
                     ple_w, ple_norm, ple_gate_w, ple_gate_b, final_norm)
    return (y_prompt, y_sample)
```

```python
import functools
import math

import jax
import jax.numpy as jnp
from jax import lax
from jax.experimental import pallas as pl
from jax.experimental.pallas import tpu as pltpu

F32 = jnp.float32
BF16 = jnp.bfloat16

D_MODEL = 1024
ATTN_WIDTH = 512
GMLP_WIDTH = 512
HEAD_DIM = 64
HEAD_WIDTH = 2 * HEAD_DIM
N_HEADS = ATTN_WIDTH // HEAD_WIDTH
ROT_DIM = HEAD_DIM // 4
ROPE_THETA = 500000.0
CHUNK = 128
N_GROUPS = 4
D_FF = 3584
N_EXPERTS = 8
TOP_K = 2
PLE_DIM = 256
EPS = 1e-6
LANES = 128
Q_SCALE = (1.0 / math.sqrt(HEAD_DIM)) * math.log2(math.e)
VMEM_LIMIT = 48 * 1024 * 1024


def _tile(n, pref):
    t = min(n, pref)
    while n % t:
        t //= 2
    return t


def _params(sem):
    return pltpu.CompilerParams(dimension_semantics=sem, vmem_limit_bytes=VMEM_LIMIT)


def _rms(x):
    return x * lax.rsqrt(jnp.mean(x * x, axis=-1, keepdims=True) + EPS)


def _inproj_kernel(x_ref, g_ref, w_ref, cos_ref, sa_ref, sb_ref, lng_ref, lnb_ref, ws_ref, bs_ref,
                   q_ref, k_ref, v_ref, gm_ref):
    tm = x_ref.shape[0]
    h = (_rms(x_ref[...]) * g_ref[...]).astype(BF16)

    def proj(c0, n):
        return jnp.dot(h, w_ref[:, c0:c0 + n], preferred_element_type=F32)

    cos, sa, sb = cos_ref[...], sa_ref[...], sb_ref[...]
    half = ROT_DIM // 2

    def rope(t):
        outs = []
        for hh in range(N_HEADS):
            th = t[:, hh * HEAD_WIDTH:(hh + 1) * HEAD_WIDTH]
            outs.append(th * cos + pltpu.roll(th, half, 1) * sa
                        + pltpu.roll(th, HEAD_WIDTH - half, 1) * sb)
        return jnp.concatenate(outs, axis=1)

    q_ref[...] = (rope(proj(0, ATTN_WIDTH)) * Q_SCALE).astype(BF16)
    k_ref[...] = rope(proj(ATTN_WIDTH, ATTN_WIDTH)).astype(BF16)
    v_ref[...] = proj(2 * ATTN_WIDTH, ATTN_WIDTH).astype(BF16)

    u = jax.nn.gelu(proj(3 * ATTN_WIDTH, GMLP_WIDTH))
    vf = jax.nn.gelu(proj(3 * ATTN_WIDTH + GMLP_WIDTH, GMLP_WIDTH))
    mu = jnp.mean(vf, axis=-1, keepdims=True)
    vc = vf - mu
    var = jnp.mean(vc * vc, axis=-1, keepdims=True)
    vn = (vc * lax.rsqrt(var + EPS) * lng_ref[...] + lnb_ref[...]).astype(BF16)
    for c in range(tm // CHUNK):
        r = slice(c * CHUNK, (c + 1) * CHUNK)
        for g in range(N_GROUPS):
            cg = slice(g * CHUNK, (g + 1) * CHUNK)
            mixed = jnp.dot(ws_ref[g], vn[r, cg], preferred_element_type=F32) + bs_ref[g]
            gm_ref[r, cg] = (u[r, cg] * mixed).astype(BF16)


def _inproj(x, g, w, cos, sa, sb, lng, lnb, ws, bsb, seq):
    T = x.shape[0]
    tm = _tile(seq, 512)
    ns = seq // tm
    full = lambda i: (0, 0)
    out = jax.ShapeDtypeStruct((T, ATTN_WIDTH), BF16)
    return pl.pallas_call(
        _inproj_kernel,
        grid=(T // tm,),
        in_specs=[
            pl.BlockSpec((tm, D_MODEL), lambda i: (i, 0)),
            pl.BlockSpec((1, D_MODEL), full),
            pl.BlockSpec(w.shape, full),
            pl.BlockSpec((tm, HEAD_WIDTH), lambda i: (i % ns, 0)),
            pl.BlockSpec((tm, HEAD_WIDTH), lambda i: (i % ns, 0)),
            pl.BlockSpec((tm, HEAD_WIDTH), lambda i: (i % ns, 0)),
            pl.BlockSpec((1, GMLP_WIDTH), full),
            pl.BlockSpec((1, GMLP_WIDTH), full),
            pl.BlockSpec(ws.shape, lambda i: (0, 0, 0)),
            pl.BlockSpec(bsb.shape, lambda i: (0, 0, 0)),
        ],
        out_specs=[pl.BlockSpec((tm, ATTN_WIDTH), lambda i: (i, 0))] * 4,
        out_shape=[out] * 4,
        compiler_params=_params(("parallel",)),
        name="inproj",
    )(x, g, w, cos, sa, sb, lng, lnb, ws, bsb)


def _attn_kernel(q_ref, k_ref, v_ref, lam_ref, sub_ref, o_ref, m_ref, l_ref, acc1_ref, acc2_ref,
                 *, tk, lam_init):
    tq = q_ref.shape[0]
    seq = k_ref.shape[0]
    qt = q_ref[...].astype(F32).T
    row = lax.broadcasted_iota(jnp.int32, qt.shape, 0)
    qcat = jnp.concatenate([jnp.where(row < HEAD_DIM, qt, 0.0),
                            jnp.where(row >= HEAD_DIM, qt, 0.0)], axis=1).astype(BF16)

    m_ref[...] = jnp.full(m_ref.shape, -jnp.inf, F32)
    l_ref[...] = jnp.zeros(l_ref.shape, F32)
    acc1_ref[...] = jnp.zeros(acc1_ref.shape, F32)
    acc2_ref[...] = jnp.zeros(acc2_ref.shape, F32)

    def step(t, carry):
        r = pl.ds(pl.multiple_of(t * tk, tk), tk)
        s = jnp.dot(k_ref[r, :], qcat, preferred_element_type=F32)
        m_old = m_ref[...]
        m_new = jnp.maximum(m_old, jnp.max(s, axis=0, keepdims=True))
        alpha = jnp.exp2(m_old - m_new)
        p = jnp.exp2(s - m_new)
        l_ref[...] = alpha * l_ref[...] + jnp.sum(p, axis=0, keepdims=True)
        m_ref[...] = m_new
        p16 = p.astype(BF16)
        vt = v_ref[r, :]
        tn = (((0,), (0,)), ((), ()))
        acc1_ref[...] = alpha[:, :tq] * acc1_ref[...] + lax.dot_general(
            vt, p16[:, :tq], tn, preferred_element_type=F32)
        acc2_ref[...] = alpha[:, tq:] * acc2_ref[...] + lax.dot_general(
            vt, p16[:, tq:], tn, preferred_element_type=F32)
        return carry

    lax.fori_loop(0, seq // tk, step, 0)

    lp = lam_ref[...]
    lam = (jnp.exp(jnp.sum(lp[0:1] * lp[1:2], axis=-1, keepdims=True))
           - jnp.exp(jnp.sum(lp[2:3] * lp[3:4], axis=-1, keepdims=True)) + lam_init)
    l = l_ref[...]
    ot = acc1_ref[...] / l[:, :tq] - lam * (acc2_ref[...] / l[:, tq:])
    o = ot.T
    o_ref[...] = (_rms(o) * sub_ref[...] * (1.0 - lam_init)).astype(o_ref.dtype)


def _attention(q, k, v, lam_p, sub, batch, seq, lam_init):
    T = q.shape[0]
    tq = _tile(seq, 256)
    tk = _tile(seq, 512)
    nq = seq // tq
    kv_spec = pl.BlockSpec((seq, HEAD_WIDTH), lambda b, h, i: (b, h))
    return pl.pallas_call(
        functools.partial(_attn_kernel, tk=tk, lam_init=lam_init),
        grid=(batch, N_HEADS, nq),
        in_specs=[
            pl.BlockSpec((tq, HEAD_WIDTH), lambda b, h, i: (b * nq + i, h)),
            kv_spec,
            kv_spec,
            pl.BlockSpec(lam_p.shape, lambda b, h, i: (0, 0)),
            pl.BlockSpec((1, HEAD_WIDTH), lambda b, h, i: (0, 0)),
        ],
        out_specs=pl.BlockSpec((tq, HEAD_WIDTH), lambda b, h, i: (b * nq + i, h)),
        out_shape=jax.ShapeDtypeStruct((T, ATTN_WIDTH), BF16),
        scratch_shapes=[
            pltpu.VMEM((1, 2 * tq), F32),
            pltpu.VMEM((1, 2 * tq), F32),
            pltpu.VMEM((HEAD_WIDTH, tq), F32),
            pltpu.VMEM((HEAD_WIDTH, tq), F32),
        ],
        compiler_params=_params(("parallel", "parallel", "parallel")),
        name="attn",
    )(q, k, v, lam_p, sub)


def _outproj_kernel(att_ref, gm_ref, x_ref, w_ref, g_ref, *rest, moe):
    if moe:
        r_ref, x1_ref, h_ref, route_ref = rest
    else:
        x1_ref, h_ref = rest
    y = (jnp.dot(att_ref[...], w_ref[:ATTN_WIDTH, :], preferred_element_type=F32)
         + jnp.dot(gm_ref[...], w_ref[ATTN_WIDTH:, :], preferred_element_type=F32))
    x1 = x_ref[...] + y
    x1_ref[...] = x1
    h = _rms(x1) * g_ref[...]
    h_ref[...] = h.astype(BF16)
    if moe:
        logits = jnp.dot(h, r_ref[...], preferred_element_type=F32,
                         precision=lax.Precision.HIGHEST)
        lane = lax.broadcasted_iota(jnp.int32, logits.shape, 1)
        lg = jnp.where(lane < N_EXPERTS, logits, -jnp.inf)
        v1 = jnp.max(lg, axis=-1, keepdims=True)
        i1 = jnp.min(jnp.where(lg == v1, lane, LANES), axis=-1, keepdims=True)
        lg2 = jnp.where(lane == i1, -jnp.inf, lg)
        v2 = jnp.max(lg2, axis=-1, keepdims=True)
        i2 = jnp.min(jnp.where(lg2 == v2, lane, LANES), axis=-1, keepdims=True)
        e2 = jnp.exp(v2 - v1)
        den = 1.0 + e2
        route_ref[...] = jnp.where(
            lane == 0, i1.astype(F32),
            jnp.where(lane == 1, i2.astype(F32),
                      jnp.where(lane == 2, 1.0 / den, jnp.where(lane == 3, e2 / den, 0.0))))


def _outproj(att, gm, x, w, g, router_pad):
    T = x.shape[0]
    tm = _tile(T, 512)
    moe = router_pad is not None
    full = lambda i: (0, 0)
    row = lambda i: (i, 0)
    in_specs = [
        pl.BlockSpec((tm, ATTN_WIDTH), row),
        pl.BlockSpec((tm, GMLP_WIDTH), row),
        pl.BlockSpec((tm, D_MODEL), row),
        pl.BlockSpec(w.shape, full),
        pl.BlockSpec((1, D_MODEL), full),
    ]
    out_specs = [pl.BlockSpec((tm, D_MODEL), row), pl.BlockSpec((tm, D_MODEL), row)]
    out_shape = [jax.ShapeDtypeStruct((T, D_MODEL), F32), jax.ShapeDtypeStruct((T, D_MODEL), BF16)]
    args = [att, gm, x, w, g]
    if moe:
        in_specs.append(pl.BlockSpec(router_pad.shape, full))
        out_specs.append(pl.BlockSpec((tm, LANES), row))
        out_shape.append(jax.ShapeDtypeStruct((T, LANES), F32))
        args.append(router_pad)
    return pl.pallas_call(
        functools.partial(_outproj_kernel, moe=moe),
        grid=(T // tm,),
        in_specs=in_specs,
        out_specs=out_specs,
        out_shape=out_shape,
        compiler_params=_params(("parallel",)),
        name="outproj_moe" if moe else "outproj",
    )(*args)


def _swiglu_step(h_ref, wg_ref, wu_ref, wd_ref, acc_ref):
    j = pl.program_id(1)
    h = h_ref[...]
    a = (jax.nn.silu(jnp.dot(h, wg_ref[...], preferred_element_type=F32))
         * jnp.dot(h, wu_ref[...], preferred_element_type=F32)).astype(BF16)
    part = jnp.dot(a, wd_ref[...], preferred_element_type=F32)

    @pl.when(j == 0)
    def _():
        acc_ref[...] = part

    @pl.when(j > 0)
    def _():
        acc_ref[...] += part


def _dense_ffn_kernel(h_ref, x_ref, wg_ref, wu_ref, wd_ref, o_ref, acc_ref):
    _swiglu_step(h_ref, wg_ref, wu_ref, wd_ref, acc_ref)

    @pl.when(pl.program_id(1) == pl.num_programs(1) - 1)
    def _():
        o_ref[...] = x_ref[...] + acc_ref[...]


def _dense_ffn(h, x, wg, wu, wd):
    T = h.shape[0]
    tm = _tile(T, 512)
    tf = _tile(D_FF, 512)
    row = lambda i, j: (i, 0)
    return pl.pallas_call(
        _dense_ffn_kernel,
        grid=(T // tm, D_FF // tf),
        in_specs=[
            pl.BlockSpec((tm, D_MODEL), row),
            pl.BlockSpec((tm, D_MODEL), row),
            pl.BlockSpec((D_MODEL, tf), lambda i, j: (0, j)),
            pl.BlockSpec((D_MODEL, tf), lambda i, j: (0, j)),
            pl.BlockSpec((tf, D_MODEL), lambda i, j: (j, 0)),
        ],
        out_specs=pl.BlockSpec((tm, D_MODEL), row),
        out_shape=jax.ShapeDtypeStruct((T, D_MODEL), F32),
        scratch_shapes=[pltpu.VMEM((tm, D_MODEL), F32)],
        compiler_params=_params(("parallel", "arbitrary")),
        name="dense_ffn",
    )(h, x, wg, wu, wd)


def _moe_ffn_kernel(te_ref, nu_ref, h_ref, wg_ref, wu_ref, wd_ref, o_ref, acc_ref):
    i = pl.program_id(0)
    last = pl.program_id(1) == pl.num_programs(1) - 1

    @pl.when(i < nu_ref[0])
    def _():
        _swiglu_step(h_ref, wg_ref, wu_ref, wd_ref, acc_ref)

        @pl.when(last)
        def _():
            o_ref[...] = acc_ref[...]

    @pl.when(jnp.logical_and(i >= nu_ref[0], last))
    def _():
        o_ref[...] = jnp.zeros(o_ref.shape, o_ref.dtype)


def _moe_ffn(xs, tile_expert, n_used, wg, wu, wd, tm):
    P = xs.shape[0]
    tf = _tile(D_FF, 512)
    row = lambda i, j, te, nu: (i, 0)
    return pl.pallas_call(
        _moe_ffn_kernel,
        grid_spec=pltpu.PrefetchScalarGridSpec(
            num_scalar_prefetch=2,
            grid=(P // tm, D_FF // tf),
            in_specs=[
                pl.BlockSpec((tm, D_MODEL), row),
                pl.BlockSpec((None, D_MODEL, tf), lambda i, j, te, nu: (te[i], 0, j)),
                pl.BlockSpec((None, D_MODEL, tf), lambda i, j, te, nu: (te[i], 0, j)),
                pl.BlockSpec((None, tf, D_MODEL), lambda i, j, te, nu: (te[i], j, 0)),
            ],
            out_specs=pl.BlockSpec((tm, D_MODEL), row),
            scratch_shapes=[pltpu.VMEM((tm, D_MODEL), F32)],
        ),
        out_shape=jax.ShapeDtypeStruct((P, D_MODEL), F32),
        compiler_params=_params(("arbitrary", "arbitrary")),
        name="moe_ffn",
    )(tile_expert, n_used, xs, wg, wu, wd)


def _ple_kernel(*refs, moe, last):
    refs = list(refs)
    x_ref = refs.pop(0)
    if moe:
        y_ref, route_ref = refs.pop(0), refs.pop(0)
    p_ref, wg_ref, b_ref, pw_ref, pn_ref = refs[:5]
    refs = refs[5:]
    if last:
        fn_ref = refs.pop(0)
    o_ref = refs.pop(0)

    x = x_ref[...]
    if moe:
        rt = route_ref[...]
        x = x + (rt[:, 2:3] * y_ref[:, :D_MODEL] + rt[:, 3:4] * y_ref[:, D_MODEL:])
    gate = jax.nn.sigmoid(jnp.dot(_rms(x).astype(BF16), wg_ref[...], preferred_element_type=F32)
                          + b_ref[...])
    e = jnp.dot(p_ref[...].astype(BF16), pw_ref[...], preferred_element_type=F32)
    x = x + gate * (_rms(e) * pn_ref[...])
    if last:
        x = _rms(x) * fn_ref[...]
    o_ref[...] = x


def _ple(x, y2, route, p, wg, b, pw, pn, final_norm):
    T = x.shape[0]
    tm = _tile(T, 512)
    moe = y2 is not None
    last = final_norm is not None
    full = lambda i: (0, 0)
    row = lambda i: (i, 0)
    in_specs = [pl.BlockSpec((tm, D_MODEL), row)]
    args = [x]
    if moe:
        in_specs += [pl.BlockSpec((tm, 2 * D_MODEL), row), pl.BlockSpec((tm, LANES), row)]
        args += [y2, route]
    in_specs += [
        pl.BlockSpec((tm, PLE_DIM), row),
        pl.BlockSpec(wg.shape, full),
        pl.BlockSpec((1, D_MODEL), full),
        pl.BlockSpec(pw.shape, full),
        pl.BlockSpec((1, D_MODEL), full),
    ]
    args += [p, wg, b, pw, pn]
    if last:
        in_specs.append(pl.BlockSpec((1, D_MODEL), full))
        args.append(final_norm)
    return pl.pallas_call(
        functools.partial(_ple_kernel, moe=moe, last=last),
        grid=(T // tm,),
        in_specs=in_specs,
        out_specs=pl.BlockSpec((tm, D_MODEL), row),
        out_shape=jax.ShapeDtypeStruct((T, D_MODEL), F32),
        compiler_params=_params(("parallel",)),
        name="ple" + ("_moe" if moe else "") + ("_last" if last else ""),
    )(*args)


def _rope_tables(seq):
    half = ROT_DIM // 2
    pos = jnp.arange(seq, dtype=F32)
    inv = ROPE_THETA ** (-jnp.arange(half, dtype=F32) * 2.0 / ROT_DIM)
    ang = pos[:, None] * inv[None, :]
    cos, sin = jnp.cos(ang), jnp.sin(ang)
    ones = jnp.ones((seq, HEAD_DIM - ROT_DIM), F32)
    zeros = jnp.zeros((seq, HEAD_DIM - ROT_DIM), F32)
    zh = jnp.zeros((seq, half), F32)
    c = jnp.concatenate([cos, cos, ones], axis=1)
    sa = jnp.concatenate([zh, sin, zeros], axis=1)
    sb = jnp.concatenate([-sin, zh, zeros], axis=1)
    rep = lambda t: jnp.concatenate([t, t], axis=1)
    return rep(c), rep(sa), rep(sb)


def _dispatch(route, tm):
    T = route.shape[0]
    e_flat = route[:, :TOP_K].astype(jnp.int32).reshape(-1)
    onehot = (e_flat[:, None] == jnp.arange(N_EXPERTS, dtype=jnp.int32)[None, :]).astype(jnp.int32)
    csum = jnp.cumsum(onehot, axis=0)
    rank = jnp.sum((csum - onehot) * onehot, axis=1)
    counts = csum[-1]
    padded = ((counts + tm - 1) // tm) * tm
    ends = jnp.cumsum(padded)
    starts = ends - padded
    pos = jnp.sum(onehot * starts[None, :], axis=1) + rank
    n_tiles = (TOP_K * T + N_EXPERTS * tm) // tm
    tile_start = jnp.arange(n_tiles, dtype=jnp.int32) * tm
    tile_expert = jnp.minimum(
        jnp.sum((tile_start[:, None] >= ends[None, :]).astype(jnp.int32), axis=1), N_EXPERTS - 1)
    n_used = (ends[-1] // tm).astype(jnp.int32).reshape(1)
    src = jnp.zeros((n_tiles * tm,), jnp.int32).at[pos].set(
        jnp.arange(TOP_K * T, dtype=jnp.int32) // TOP_K)
    return pos, src, tile_expert.astype(jnp.int32), n_used


def _trunk(x, p, W):
    B, S, _ = x.shape
    T = B * S
    depth = p.shape[0]
    x = x.reshape(T, D_MODEL)
    p = p.reshape(depth, T, PLE_DIM)
    cos, sa, sb = _rope_tables(S)
    for i in range(depth):
        lam_init = 0.8 - 0.6 * math.exp(-0.3 * i)
        q, k, v, gm = _inproj(x, W["norm_mix"][i], W["w_in"][i], cos, sa, sb, W["gmlp_ln_g"][i],
                              W["gmlp_ln_b"][i], W["gmlp_ws"][i], W["gmlp_bs"][i], S)
        att = _attention(q, k, v, W["lam"][i], W["subln"][i], B, S, lam_init)
        moe = i % 2 == 1
        j = i // 2
        fin = W["final_norm"] if i == depth - 1 else None
        ple_w = (W["ple_gate_w"][i], W["ple_gate_b"][i], W["ple_w"][i], W["ple_norm"][i], fin)
        if not moe:
            x1, h2 = _outproj(att, gm, x, W["w_out"][i], W["norm_ffn"][i], None)
            x2 = _dense_ffn(h2, x1, W["dense_w_gate"][j], W["dense_w_up"][j], W["dense_w_down"][j])
            x = _ple(x2, None, None, p[i], *ple_w)
        else:
            x1, h2, route = _outproj(att, gm, x, W["w_out"][i], W["norm_ffn"][i], W["router"][j])
            tm = _tile(T, 512)
            pos, src, tile_expert, n_used = _dispatch(route, tm)
            xs = jnp.take(h2, src, axis=0)
            ys = _moe_ffn(xs, tile_expert, n_used, W["moe_w_gate"][j], W["moe_w_up"][j],
                          W["moe_w_down"][j], tm)
            y2 = jnp.take(ys, pos, axis=0).reshape(T, TOP_K * D_MODEL)
            x = _ple(x1, y2, route, p[i], *ple_w)
    return x.reshape(B, S, D_MODEL)


def kernel(x_prompt, x_sample, p_prompt, p_sample, norm_mix, w_in, lambda_q1, lambda_k1, lambda_q2, lambda_k2, subln, gmlp_ln_g, gmlp_ln_b, gmlp_ws, gmlp_bs, w_out, norm_ffn, dense_w_gate, dense_w_up, dense_w_down, router, moe_w_gate, moe_w_up, moe_w_down, ple_w, ple_norm, ple_gate_w, ple_gate_b, final_norm):
    depth = w_in.shape[0]
    row = lambda t: t[:, None, :]
    lam = jnp.stack([lambda_q1, lambda_k1, lambda_q2, lambda_k2], axis=1)
    lam = jnp.pad(lam, ((0, 0), (0, 4), (0, LANES - HEAD_DIM)))
    W = dict(
        norm_mix=row(norm_mix), w_in=w_in.astype(BF16), lam=lam, subln=row(subln),
        gmlp_ln_g=row(gmlp_ln_g), gmlp_ln_b=row(gmlp_ln_b), gmlp_ws=gmlp_ws.astype(BF16),
        gmlp_bs=jnp.broadcast_to(gmlp_bs[..., None], gmlp_bs.shape + (CHUNK,)),
        w_out=w_out.astype(BF16), norm_ffn=row(norm_ffn),
        dense_w_gate=dense_w_gate.astype(BF16), dense_w_up=dense_w_up.astype(BF16),
        dense_w_down=dense_w_down.astype(BF16),
        router=jnp.pad(router, ((0, 0), (0, 0), (0, LANES - N_EXPERTS))),
        moe_w_gate=moe_w_gate.astype(BF16), moe_w_up=moe_w_up.astype(BF16),
        moe_w_down=moe_w_down.astype(BF16),
        ple_w=ple_w.astype(BF16), ple_norm=row(ple_norm), ple_gate_w=ple_gate_w.astype(BF16),
        ple_gate_b=row(ple_gate_b), final_norm=final_norm[None, :],
    )
    return (_trunk(x_prompt, p_prompt, W), _trunk(x_sample, p_sample, W))
```

```python
import functools
import math

import jax
import jax.numpy as jnp
from jax import lax
from jax.experimental import pallas as pl
from jax.experimental.pallas import tpu as pltpu

F32 = jnp.float32
BF16 = jnp.bfloat16

D_MODEL = 1024
ATTN_WIDTH = 512
GMLP_WIDTH = 512
HEAD_DIM = 64
HEAD_WIDTH = 2 * HEAD_DIM
N_HEADS = ATTN_WIDTH // HEAD_WIDTH
ROT_DIM = HEAD_DIM // 4
ROPE_THETA = 500000.0
CHUNK = 128
N_GROUPS = 4
D_FF = 3584
N_EXPERTS = 8
TOP_K = 2
PLE_DIM = 256
EPS = 1e-6
LANES = 128
Q_SCALE = (1.0 / math.sqrt(HEAD_DIM)) * math.log2(math.e)
VMEM_LIMIT = 48 * 1024 * 1024
ATTN_TQ = 512
ATTN_TK = 512
ATTN_COLS = 256


def _tile(n, pref):
    t = min(n, pref)
    while n % t:
        t //= 2
    return t


def _params(sem):
    return pltpu.CompilerParams(dimension_semantics=sem, vmem_limit_bytes=VMEM_LIMIT)


def _rms(x):
    return x * lax.rsqrt(jnp.mean(x * x, axis=-1, keepdims=True) + EPS)


def _inproj_kernel(x_ref, g_ref, w_ref, cos_ref, sa_ref, sb_ref, lng_ref, lnb_ref, ws_ref, bs_ref,
                   q_ref, k_ref, v_ref, gm_ref):
    tm = x_ref.shape[0]
    h = (_rms(x_ref[...]) * g_ref[...]).astype(BF16)

    def proj(c0, n):
        return jnp.dot(h, w_ref[:, c0:c0 + n], preferred_element_type=F32)

    cos, sa, sb = cos_ref[...], sa_ref[...], sb_ref[...]
    half = ROT_DIM // 2

    def rope(t):
        outs = []
        for hh in range(N_HEADS):
            th = t[:, hh * HEAD_WIDTH:(hh + 1) * HEAD_WIDTH]
            outs.append(th * cos + pltpu.roll(th, half, 1) * sa
                        + pltpu.roll(th, HEAD_WIDTH - half, 1) * sb)
        return jnp.concatenate(outs, axis=1)

    q_ref[...] = (rope(proj(0, ATTN_WIDTH)) * Q_SCALE).astype(BF16)
    k_ref[...] = rope(proj(ATTN_WIDTH, ATTN_WIDTH)).astype(BF16)
    v_ref[...] = proj(2 * ATTN_WIDTH, ATTN_WIDTH).astype(BF16)

    u = jax.nn.gelu(proj(3 * ATTN_WIDTH, GMLP_WIDTH))
    vf = jax.nn.gelu(proj(3 * ATTN_WIDTH + GMLP_WIDTH, GMLP_WIDTH))
    mu = jnp.mean(vf, axis=-1, keepdims=True)
    vc = vf - mu
    var = jnp.mean(vc * vc, axis=-1, keepdims=True)
    vn = (vc * lax.rsqrt(var + EPS) * lng_ref[...] + lnb_ref[...]).astype(BF16)
    for c in range(tm // CHUNK):
        r = slice(c * CHUNK, (c + 1) * CHUNK)
        for g in range(N_GROUPS):
            cg = slice(g * CHUNK, (g + 1) * CHUNK)
            mixed = jnp.dot(ws_ref[g], vn[r, cg], preferred_element_type=F32) + bs_ref[g]
            gm_ref[r, cg] = (u[r, cg] * mixed).astype(BF16)


def _inproj(x, g, w, cos, sa, sb, lng, lnb, ws, bsb, seq):
    T = x.shape[0]
    tm = _tile(seq, 512)
    ns = seq // tm
    full = lambda i: (0, 0)
    out = jax.ShapeDtypeStruct((T, ATTN_WIDTH), BF16)
    return pl.pallas_call(
        _inproj_kernel,
        grid=(T // tm,),
        in_specs=[
            pl.BlockSpec((tm, D_MODEL), lambda i: (i, 0)),
            pl.BlockSpec((1, D_MODEL), full),
            pl.BlockSpec(w.shape, full),
            pl.BlockSpec((tm, HEAD_WIDTH), lambda i: (i % ns, 0)),
            pl.BlockSpec((tm, HEAD_WIDTH), lambda i: (i % ns, 0)),
            pl.BlockSpec((tm, HEAD_WIDTH), lambda i: (i % ns, 0)),
            pl.BlockSpec((1, GMLP_WIDTH), full),
            pl.BlockSpec((1, GMLP_WIDTH), full),
            pl.BlockSpec(ws.shape, lambda i: (0, 0, 0)),
            pl.BlockSpec(bsb.shape, lambda i: (0, 0, 0)),
        ],
        out_specs=[pl.BlockSpec((tm, ATTN_WIDTH), lambda i: (i, 0))] * 4,
        out_shape=[out] * 4,
        compiler_params=_params(("parallel",)),
        name="inproj",
    )(x, g, w, cos, sa, sb, lng, lnb, ws, bsb)


def _attn_kernel(q_ref, k_ref, v_ref, lam_ref, sub_ref, o_ref,
                 qcat_ref, sa_ref, sb_ref, m_ref, l_ref, acc_ref, *, tk, lam_init):
    tq = q_ref.shape[0]
    n = k_ref.shape[0] // tk
    qt = q_ref[...].astype(F32).T
    row = lax.broadcasted_iota(jnp.int32, qt.shape, 0)
    qcat_ref[:, :tq] = jnp.where(row < HEAD_DIM, qt, 0.0).astype(BF16)
    qcat_ref[:, tq:] = jnp.where(row >= HEAD_DIM, qt, 0.0).astype(BF16)
    m_ref[...] = jnp.full(m_ref.shape, -jnp.inf, F32)
    l_ref[...] = jnp.zeros(l_ref.shape, F32)
    acc_ref[...] = jnp.zeros(acc_ref.shape, F32)

    def rows(t):
        return pl.ds(pl.multiple_of(t * tk, tk), tk)

    def scores(t, s_ref):
        s_ref[...] = jnp.dot(k_ref[rows(t), :], qcat_ref[...], preferred_element_type=F32)

    def update(t, s_ref):
        vt = v_ref[rows(t), :]
        for cb in range(2 * tq // ATTN_COLS):
            cols = slice(cb * ATTN_COLS, (cb + 1) * ATTN_COLS)
            s = s_ref[:, cols]
            m_old = m_ref[:, cols]
            m_new = jnp.maximum(m_old, jnp.max(s, axis=0, keepdims=True))
            alpha = jnp.exp2(m_old - m_new)
            p = jnp.exp2(s - m_new)
            l_ref[:, cols] = alpha * l_ref[:, cols] + jnp.sum(p, axis=0, keepdims=True)
            m_ref[:, cols] = m_new
            pv = lax.dot_general(vt, p.astype(BF16), (((0,), (0,)), ((), ())),
                                 preferred_element_type=F32)
            acc_ref[:, cols] = alpha * acc_ref[:, cols] + pv

    scores(0, sa_ref)

    def pair(u, carry):
        t = 2 * u
        scores(t + 1, sb_ref)
        update(t, sa_ref)
        scores(t + 2, sa_ref)
        update(t + 1, sb_ref)
        return carry

    lax.fori_loop(0, n // 2 - 1, pair, 0)
    scores(n - 1, sb_ref)
    update(n - 2, sa_ref)
    update(n - 1, sb_ref)

    lp = lam_ref[...]
    lam = (jnp.exp(jnp.sum(lp[0:1] * lp[1:2], axis=-1, keepdims=True))
           - jnp.exp(jnp.sum(lp[2:3] * lp[3:4], axis=-1, keepdims=True)) + lam_init)
    l = l_ref[...]
    ot = acc_ref[:, :tq] / l[:, :tq] - lam * (acc_ref[:, tq:] / l[:, tq:])
    o = ot.T
    o_ref[...] = (_rms(o) * sub_ref[...] * (1.0 - lam_init)).astype(o_ref.dtype)


def _attention(q, k, v, lam_p, sub, batch, seq, lam_init):
    T = q.shape[0]
    tq = _tile(seq, ATTN_TQ)
    tk = _tile(seq // 2, ATTN_TK)
    nq = seq // tq
    kv_spec = pl.BlockSpec((seq, HEAD_WIDTH), lambda b, h, i: (b, h))
    return pl.pallas_call(
        functools.partial(_attn_kernel, tk=tk, lam_init=lam_init),
        grid=(batch, N_HEADS, nq),
        in_specs=[
            pl.BlockSpec((tq, HEAD_WIDTH), lambda b, h, i: (b * nq + i, h)),
            kv_spec,
            kv_spec,
            pl.BlockSpec(lam_p.shape, lambda b, h, i: (0, 0)),
            pl.BlockSpec((1, HEAD_WIDTH), lambda b, h, i: (0, 0)),
        ],
        out_specs=pl.BlockSpec((tq, HEAD_WIDTH), lambda b, h, i: (b * nq + i, h)),
        out_shape=jax.ShapeDtypeStruct((T, ATTN_WIDTH), BF16),
        scratch_shapes=[
            pltpu.VMEM((HEAD_WIDTH, 2 * tq), BF16),
            pltpu.VMEM((tk, 2 * tq), F32),
            pltpu.VMEM((tk, 2 * tq), F32),
            pltpu.VMEM((1, 2 * tq), F32),
            pltpu.VMEM((1, 2 * tq), F32),
            pltpu.VMEM((HEAD_WIDTH, 2 * tq), F32),
        ],
        compiler_params=_params(("parallel", "parallel", "parallel")),
        name="attn",
    )(q, k, v, lam_p, sub)


def _pack_bf16_pairs(h):
    half = D_MODEL // 2
    h16 = h.astype(BF16)
    lo = lax.bitcast_convert_type(h16[:, :half].astype(F32), jnp.uint32)
    hi = lax.bitcast_convert_type(h16[:, half:].astype(F32), jnp.uint32)
    return (lo >> 16) | (hi & jnp.uint32(0xFFFF0000))


def _unpack_bf16_pairs(xp):
    lo = lax.bitcast_convert_type(xp << 16, F32)
    hi = lax.bitcast_convert_type(xp & jnp.uint32(0xFFFF0000), F32)
    return lo.astype(BF16), hi.astype(BF16)


def _outproj_kernel(att_ref, gm_ref, x_ref, w_ref, g_ref, *rest, moe):
    if moe:
        r_ref, x1_ref, h_ref, route_ref, count_ref, cnt_ref = rest
    else:
        x1_ref, h_ref = rest
    y = (jnp.dot(att_ref[...], w_ref[:ATTN_WIDTH, :], preferred_element_type=F32)
         + jnp.dot(gm_ref[...], w_ref[ATTN_WIDTH:, :], preferred_element_type=F32))
    x1 = x_ref[...] + y
    x1_ref[...] = x1
    h = _rms(x1) * g_ref[...]
    if not moe:
        h_ref[...] = h.astype(BF16)
        return
    h_ref[...] = _pack_bf16_pairs(h)
    tm = h.shape[0]
    logits = jnp.dot(h, r_ref[...], preferred_element_type=F32,
                     precision=lax.Precision.HIGHEST)
    lane = lax.broadcasted_iota(jnp.int32, logits.shape, 1)
    lg = jnp.where(lane < N_EXPERTS, logits, -jnp.inf)
    v1 = jnp.max(lg, axis=-1, keepdims=True)
    i1 = jnp.min(jnp.where(lg == v1, lane, LANES), axis=-1, keepdims=True)
    lg2 = jnp.where(lane == i1, -jnp.inf, lg)
    v2 = jnp.max(lg2, axis=-1, keepdims=True)
    i2 = jnp.min(jnp.where(lg2 == v2, lane, LANES), axis=-1, keepdims=True)
    e2 = jnp.exp(v2 - v1)
    den = 1.0 + e2

    @pl.when(pl.program_id(0) == 0)
    def _():
        cnt_ref[...] = jnp.zeros(cnt_ref.shape, F32)

    sel1 = lane == i1
    sel2 = lane == i2
    onehot = jnp.where(jnp.logical_or(sel1, sel2), 1.0, 0.0)
    r_i = lax.broadcasted_iota(jnp.int32, (tm, tm), 0)
    c_i = lax.broadcasted_iota(jnp.int32, (tm, tm), 1)
    tri = jnp.where(c_i < r_i, 1.0, 0.0).astype(BF16)
    before = jnp.dot(tri, onehot.astype(BF16), preferred_element_type=F32) + cnt_ref[...]
    rank1 = jnp.sum(jnp.where(sel1, before, 0.0), axis=-1, keepdims=True)
    rank2 = jnp.sum(jnp.where(sel2, before, 0.0), axis=-1, keepdims=True)
    cnt = cnt_ref[...] + jnp.sum(onehot, axis=0, keepdims=True)
    cnt_ref[...] = cnt
    count_ref[...] = jnp.broadcast_to(cnt, count_ref.shape)
    route = jnp.zeros(logits.shape, F32)
    for col, val in enumerate((i1.astype(F32), i2.astype(F32), 1.0 / den, e2 / den, rank1, rank2)):
        route = jnp.where(lane == col, val, route)
    route_ref[...] = route


def _outproj(att, gm, x, w, g, router_pad):
    T = x.shape[0]
    tm = _tile(T, 512)
    moe = router_pad is not None
    full = lambda i: (0, 0)
    row = lambda i: (i, 0)
    in_specs = [
        pl.BlockSpec((tm, ATTN_WIDTH), row),
        pl.BlockSpec((tm, GMLP_WIDTH), row),
        pl.BlockSpec((tm, D_MODEL), row),
        pl.BlockSpec(w.shape, full),
        pl.BlockSpec((1, D_MODEL), full),
    ]
    args = [att, gm, x, w, g]
    if not moe:
        out_specs = [pl.BlockSpec((tm, D_MODEL), row), pl.BlockSpec((tm, D_MODEL), row)]
        out_shape = [jax.ShapeDtypeStruct((T, D_MODEL), F32),
                     jax.ShapeDtypeStruct((T, D_MODEL), BF16)]
        scratch = []
    else:
        in_specs.append(pl.BlockSpec(router_pad.shape, full))
        args.append(router_pad)
        out_specs = [pl.BlockSpec((tm, D_MODEL), row), pl.BlockSpec((tm, D_MODEL // 2), row),
                     pl.BlockSpec((tm, LANES), row), pl.BlockSpec((8, LANES), full)]
        out_shape = [jax.ShapeDtypeStruct((T, D_MODEL), F32),
                     jax.ShapeDtypeStruct((T, D_MODEL // 2), jnp.uint32),
                     jax.ShapeDtypeStruct((T, LANES), F32),
                     jax.ShapeDtypeStruct((8, LANES), F32)]
        scratch = [pltpu.VMEM((1, LANES), F32)]
    return pl.pallas_call(
        functools.partial(_outproj_kernel, moe=moe),
        grid=(T // tm,),
        in_specs=in_specs,
        out_specs=out_specs,
        out_shape=out_shape,
        scratch_shapes=scratch,
        compiler_params=_params(("arbitrary",) if moe else ("parallel",)),
        name="outproj_moe" if moe else "outproj",
    )(*args)


def _dispatch_kernel(dst_ref, h_ref, xs_in_ref, xs_ref, sem):
    del xs_in_ref
    tg = h_ref.shape[0]

    def row_copy(r, k):
        return pltpu.make_async_copy(h_ref.at[pl.ds(r, 1), :],
                                     xs_ref.at[pl.ds(dst_ref[TOP_K * r + k], 1), :], sem)

    def issue(r, carry):
        for k in range(TOP_K):
            row_copy(r, k).start()
        return carry

    lax.fori_loop(0, tg, issue, 0, unroll=8)
    for k in range(TOP_K):
        pltpu.make_async_copy(h_ref, xs_ref.at[pl.ds(0, tg), :], sem).wait()


def _moe_dispatch(hp, dst, n_rows):
    T = hp.shape[0]
    tg = _tile(T, 512)
    xs0 = jnp.zeros((n_rows, D_MODEL // 2), jnp.uint32)
    return pl.pallas_call(
        _dispatch_kernel,
        grid=(T // tg,),
        in_specs=[
            pl.BlockSpec((TOP_K * tg,), lambda i: (i,), memory_space=pltpu.SMEM),
            pl.BlockSpec((tg, D_MODEL // 2), lambda i: (i, 0)),
            pl.BlockSpec(memory_space=pl.ANY),
        ],
        out_specs=pl.BlockSpec(memory_space=pl.ANY),
        out_shape=jax.ShapeDtypeStruct(xs0.shape, xs0.dtype),
        scratch_shapes=[pltpu.SemaphoreType.DMA(())],
        input_output_aliases={2: 0},
        compiler_params=_params(("arbitrary",)),
        name="moe_dispatch",
    )(dst, hp, xs0)


def _swiglu_step(h_ref, wg_ref, wu_ref, wd_ref, acc_ref):
    j = pl.program_id(1)
    h = h_ref[...]
    a = (jax.nn.silu(jnp.dot(h, wg_ref[...], preferred_element_type=F32))
         * jnp.dot(h, wu_ref[...], preferred_element_type=F32)).astype(BF16)
    part = jnp.dot(a, wd_ref[...], preferred_element_type=F32)

    @pl.when(j == 0)
    def _():
        acc_ref[...] = part

    @pl.when(j > 0)
    def _():
        acc_ref[...] += part


def _dense_ffn_kernel(h_ref, x_ref, wg_ref, wu_ref, wd_ref, o_ref, acc_ref):
    _swiglu_step(h_ref, wg_ref, wu_ref, wd_ref, acc_ref)

    @pl.when(pl.program_id(1) == pl.num_programs(1) - 1)
    def _():
        o_ref[...] = x_ref[...] + acc_ref[...]


def _dense_ffn(h, x, wg, wu, wd):
    T = h.shape[0]
    tm = _tile(T, 512)
    tf = _tile(D_FF, 512)
    row = lambda i, j: (i, 0)
    return pl.pallas_call(
        _dense_ffn_kernel,
        grid=(T // tm, D_FF // tf),
        in_specs=[
            pl.BlockSpec((tm, D_MODEL), row),
            pl.BlockSpec((tm, D_MODEL), row),
            pl.BlockSpec((D_MODEL, tf), lambda i, j: (0, j)),
            pl.BlockSpec((D_MODEL, tf), lambda i, j: (0, j)),
            pl.BlockSpec((tf, D_MODEL), lambda i, j: (j, 0)),
        ],
        out_specs=pl.BlockSpec((tm, D_MODEL), row),
        out_shape=jax.ShapeDtypeStruct((T, D_MODEL), F32),
        scratch_shapes=[pltpu.VMEM((tm, D_MODEL), F32)],
        compiler_params=_params(("parallel", "arbitrary")),
        name="dense_ffn",
    )(h, x, wg, wu, wd)


def _moe_ffn_kernel(te_ref, nu_ref, xp_ref, wg_ref, wu_ref, wd_ref, o_ref, acc_ref, h_ref):
    i = pl.program_id(0)
    last = pl.program_id(1) == pl.num_programs(1) - 1

    @pl.when(i < nu_ref[0])
    def _():
        @pl.when(pl.program_id(1) == 0)
        def _():
            lo, hi = _unpack_bf16_pairs(xp_ref[...])
            h_ref[:, :D_MODEL // 2] = lo
            h_ref[:, D_MODEL // 2:] = hi

        _swiglu_step(h_ref, wg_ref, wu_ref, wd_ref, acc_ref)

        @pl.when(last)
        def _():
            o_ref[...] = acc_ref[...]

    @pl.when(jnp.logical_and(i >= nu_ref[0], last))
    def _():
        o_ref[...] = jnp.zeros(o_ref.shape, o_ref.dtype)


def _moe_ffn(xs, tile_expert, n_used, wg, wu, wd, tm):
    P = xs.shape[0]
    tf = _tile(D_FF, 512)
    row = lambda i, j, te, nu: (i, 0)
    return pl.pallas_call(
        _moe_ffn_kernel,
        grid_spec=pltpu.PrefetchScalarGridSpec(
            num_scalar_prefetch=2,
            grid=(P // tm, D_FF // tf),
            in_specs=[
                pl.BlockSpec((tm, D_MODEL // 2), row),
                pl.BlockSpec((None, D_MODEL, tf), lambda i, j, te, nu: (te[i], 0, j)),
                pl.BlockSpec((None, D_MODEL, tf), lambda i, j, te, nu: (te[i], 0, j)),
                pl.BlockSpec((None, tf, D_MODEL), lambda i, j, te, nu: (te[i], j, 0)),
            ],
            out_specs=pl.BlockSpec((tm, D_MODEL), row),
            scratch_shapes=[pltpu.VMEM((tm, D_MODEL), F32), pltpu.VMEM((tm, D_MODEL), BF16)],
        ),
        out_shape=jax.ShapeDtypeStruct((P, D_MODEL), F32),
        compiler_params=_params(("arbitrary", "arbitrary")),
        name="moe_ffn",
    )(tile_expert, n_used, xs, wg, wu, wd)


def _ple_kernel(*refs, moe, last):
    refs = list(refs)
    x_ref = refs.pop(0)
    if moe:
        dst_ref, ys_ref, route_ref = refs.pop(0), refs.pop(0), refs.pop(0)
    p_ref, wg_ref, b_ref, pw_ref, pn_ref = refs[:5]
    refs = refs[5:]
    if last:
        fn_ref = refs.pop(0)
    o_ref = refs.pop(0)
    tm = x_ref.shape[0]

    if moe:
        ybuf_ref, sem = refs

        def row_copy(r, k):
            return pltpu.make_async_copy(ys_ref.at[pl.ds(dst_ref[TOP_K * r + k], 1), :],
                                         ybuf_ref.at[k, pl.ds(r, 1), :], sem)

        def issue(r, carry):
            for k in range(TOP_K):
                row_copy(r, k).start()
            return carry

        lax.fori_loop(0, tm, issue, 0, unroll=8)

    e = jnp.dot(p_ref[...].astype(BF16), pw_ref[...], preferred_element_type=F32)
    e = _rms(e) * pn_ref[...]
    x = x_ref[...]
    if moe:
        for k in range(TOP_K):
            pltpu.make_async_copy(ys_ref.at[pl.ds(0, tm), :], ybuf_ref.at[k], sem).wait()
        rt = route_ref[...]
        x = x + (rt[:, 2:3] * ybuf_ref[0] + rt[:, 3:4] * ybuf_ref[1])
    gate = jax.nn.sigmoid(jnp.dot(_rms(x).astype(BF16), wg_ref[...], preferred_element_type=F32)
                          + b_ref[...])
    x = x + gate * e
    if last:
        x = _rms(x) * fn_ref[...]
    o_ref[...] = x


def _ple(x, dst, ys, route, p, wg, b, pw, pn, final_norm):
    T = x.shape[0]
    tm = _tile(T, 512)
    moe = ys is not None
    last = final_norm is not None
    full = lambda i: (0, 0)
    row = lambda i: (i, 0)
    in_specs = [pl.BlockSpec((tm, D_MODEL), row)]
    args = [x]
    scratch = []
    if moe:
        in_specs += [pl.BlockSpec((TOP_K * tm,), lambda i: (i,), memory_space=pltpu.SMEM),
                     pl.BlockSpec(memory_space=pl.ANY),
                     pl.BlockSpec((tm, LANES), row)]
        args += [dst, ys, route]
        scratch = [pltpu.VMEM((TOP_K, tm, D_MODEL), F32), pltpu.SemaphoreType.DMA(())]
    in_specs += [
        pl.BlockSpec((tm, PLE_DIM), row),
        pl.BlockSpec(wg.shape, full),
        pl.BlockSpec((1, D_MODEL), full),
        pl.BlockSpec(pw.shape, full),
        pl.BlockSpec((1, D_MODEL), full),
    ]
    args += [p, wg, b, pw, pn]
    if last:
        in_specs.append(pl.BlockSpec((1, D_MODEL), full))
        args.append(final_norm)
    return pl.pallas_call(
        functools.partial(_ple_kernel, moe=moe, last=last),
        grid=(T // tm,),
        in_specs=in_specs,
        out_specs=pl.BlockSpec((tm, D_MODEL), row),
        out_shape=jax.ShapeDtypeStruct((T, D_MODEL), F32),
        scratch_shapes=scratch,
        compiler_params=_params(("parallel",)),
        name="ple" + ("_moe" if moe else "") + ("_last" if last else ""),
    )(*args)


def _rope_tables(seq):
    half = ROT_DIM // 2
    pos = jnp.arange(seq, dtype=F32)
    inv = ROPE_THETA ** (-jnp.arange(half, dtype=F32) * 2.0 / ROT_DIM)
    ang = pos[:, None] * inv[None, :]
    cos, sin = jnp.cos(ang), jnp.sin(ang)
    ones = jnp.ones((seq, HEAD_DIM - ROT_DIM), F32)
    zeros = jnp.zeros((seq, HEAD_DIM - ROT_DIM), F32)
    zh = jnp.zeros((seq, half), F32)
    c = jnp.concatenate([cos, cos, ones], axis=1)
    sa = jnp.concatenate([zh, sin, zeros], axis=1)
    sb = jnp.concatenate([-sin, zh, zeros], axis=1)
    rep = lambda t: jnp.concatenate([t, t], axis=1)
    return rep(c), rep(sa), rep(sb)


def _group_layout(route, counts, tm):
    T = route.shape[0]
    counts = counts[0, :N_EXPERTS].astype(jnp.int32)
    n_tile_e = (counts + tm - 1) // tm
    tile_end = jnp.cumsum(n_tile_e)
    start = (tile_end - n_tile_e) * tm
    expert = route[:, 0:TOP_K].astype(jnp.int32)
    rank = route[:, 4:4 + TOP_K].astype(jnp.int32)
    onehot = expert[..., None] == jnp.arange(N_EXPERTS, dtype=jnp.int32)
    dst = (jnp.sum(jnp.where(onehot, start, 0), axis=-1) + rank).reshape(-1)
    n_tiles = TOP_K * T // tm + N_EXPERTS
    tile_expert = jnp.minimum(
        jnp.sum((jnp.arange(n_tiles, dtype=jnp.int32)[:, None] >= tile_end[None, :])
                .astype(jnp.int32), axis=1), N_EXPERTS - 1)
    return dst, tile_expert, tile_end[-1:], n_tiles * tm


def _trunk(x, p, W):
    B, S, _ = x.shape
    T = B * S
    depth = p.shape[0]
    x = x.reshape(T, D_MODEL)
    p = p.reshape(depth, T, PLE_DIM)
    cos, sa, sb = _rope_tables(S)
    for i in range(depth):
        lam_init = 0.8 - 0.6 * math.exp(-0.3 * i)
        q, k, v, gm = _inproj(x, W["norm_mix"][i], W["w_in"][i], cos, sa, sb, W["gmlp_ln_g"][i],
                              W["gmlp_ln_b"][i], W["gmlp_ws"][i], W["gmlp_bs"][i], S)
        att = _attention(q, k, v, W["lam"][i], W["subln"][i], B, S, lam_init)
        moe = i % 2 == 1
        j = i // 2
        fin = W["final_norm"] if i == depth - 1 else None
        ple_w = (W["ple_gate_w"][i], W["ple_gate_b"][i], W["ple_w"][i], W["ple_norm"][i], fin)
        if not moe:
            x1, h2 = _outproj(att, gm, x, W["w_out"][i], W["norm_ffn"][i], None)
            x2 = _dense_ffn(h2, x1, W["dense_w_gate"][j], W["dense_w_up"][j], W["dense_w_down"][j])
            x = _ple(x2, None, None, None, p[i], *ple_w)
        else:
            x1, hp, route, counts = _outproj(att, gm, x, W["w_out"][i], W["norm_ffn"][i],
                                             W["router"][j])
            tm = _tile(T, 512)
            dst, tile_expert, n_used, n_rows = _group_layout(route, counts, tm)
            xs = _moe_dispatch(hp, dst, n_rows)
            ys = _moe_ffn(xs, tile_expert, n_used, W["moe_w_gate"][j], W["moe_w_up"][j],
                          W["moe_w_down"][j], tm)
            x = _ple(x1, dst, ys, route, p[i], *ple_w)
    return x.reshape(B, S, D_MODEL)


def kernel(x_prompt, x_sample, p_prompt, p_sample, norm_mix, w_in, lambda_q1, lambda_k1, lambda_q2, lambda_k2, subln, gmlp_ln_g, gmlp_ln_b, gmlp_ws, gmlp_bs, w_out, norm_ffn, dense_w_gate, dense_w_up, dense_w_down, router, moe_w_gate, moe_w_up, moe_w_down, ple_w, ple_norm, ple_gate_w, ple_gate_b, final_norm):
    depth = w_in.shape[0]
    row = lambda t: t[:, None, :]
    lam = jnp.stack([lambda_q1, lambda_k1, lambda_q2, lambda_k2], axis=1)
    lam = jnp.pad(lam, ((0, 0), (0, 4), (0, LANES - HEAD_DIM)))
    W = dict(
        norm_mix=row(norm_mix), w_in=w_in.astype(BF16), lam=lam, subln=row(subln),
        gmlp_ln_g=row(gmlp_ln_g), gmlp_ln_b=row(gmlp_ln_b), gmlp_ws=gmlp_ws.astype(BF16),
        gmlp_bs=jnp.broadcast_to(gmlp_bs[..., None], gmlp_bs.shape + (CHUNK,)),
        w_out=w_out.astype(BF16), norm_ffn=row(norm_ffn),
        dense_w_gate=dense_w_gate.astype(BF16), dense_w_up=dense_w_up.astype(BF16),
        dense_w_down=dense_w_down.astype(BF16),
        router=jnp.pad(router, ((0, 0), (0, 0), (0, LANES - N_EXPERTS))),
        moe_w_gate=moe_w_gate.astype(BF16), moe_w_up=moe_w_up.astype(BF16),
        moe_w_down=moe_w_down.astype(BF16),
        ple_w=ple_w.astype(BF16), ple_norm=row(ple_norm), ple_gate_w=ple_gate_w.astype(BF16),
        ple_gate_b=row(ple_gate_b), final_norm=final_norm[None, :],
    )
    return (_trunk(x_prompt, p_prompt, W), _trunk(x_sample, p_sample, W))
```

```python
import functools
import math

import jax
import jax.numpy as jnp
from jax import lax
from jax.experimental import pallas as pl
from jax.experimental.pallas import tpu as pltpu

F32 = jnp.float32
BF16 = jnp.bfloat16

D_MODEL = 1024
ATTN_WIDTH = 512
GMLP_WIDTH = 512
HEAD_DIM = 64
HEAD_WIDTH = 2 * HEAD_DIM
N_HEADS = ATTN_WIDTH // HEAD_WIDTH
ROT_DIM = HEAD_DIM // 4
ROPE_THETA = 500000.0
CHUNK = 128
N_GROUPS = 4
D_FF = 3584
N_EXPERTS = 8
TOP_K = 2
PLE_DIM = 256
EPS = 1e-6
LANES = 128
Q_SCALE = (1.0 / math.sqrt(HEAD_DIM)) * math.log2(math.e)
VMEM_LIMIT = 48 * 1024 * 1024
ATTN_TQ = 1024
ATTN_TK = 512
ATTN_COLS = 256
FFN_TM = 1024
MOE_TM = 1024
FFN_ROW_SPLIT = 2


def _tile(n, pref):
    t = min(n, pref)
    while n % t:
        t //= 2
    return t


def _params(sem):
    return pltpu.CompilerParams(dimension_semantics=sem, vmem_limit_bytes=VMEM_LIMIT)


def _rms(x):
    return x * lax.rsqrt(jnp.mean(x * x, axis=-1, keepdims=True) + EPS)


def _inproj_kernel(x_ref, g_ref, w_ref, cos_ref, sa_ref, sb_ref, lng_ref, lnb_ref, ws_ref, bs_ref,
                   q_ref, k_ref, v_ref, gm_ref):
    tm = x_ref.shape[0]
    h = (_rms(x_ref[...]) * g_ref[...]).astype(BF16)

    def proj(c0, n):
        return jnp.dot(h, w_ref[:, c0:c0 + n], preferred_element_type=F32)

    cos, sa, sb = cos_ref[...], sa_ref[...], sb_ref[...]
    half = ROT_DIM // 2

    def rope(t):
        outs = []
        for hh in range(N_HEADS):
            th = t[:, hh * HEAD_WIDTH:(hh + 1) * HEAD_WIDTH]
            outs.append(th * cos + pltpu.roll(th, half, 1) * sa
                        + pltpu.roll(th, HEAD_WIDTH - half, 1) * sb)
        return jnp.concatenate(outs, axis=1)

    q_ref[...] = (rope(proj(0, ATTN_WIDTH)) * Q_SCALE).astype(BF16)
    k_ref[...] = rope(proj(ATTN_WIDTH, ATTN_WIDTH)).astype(BF16)
    v_ref[...] = proj(2 * ATTN_WIDTH, ATTN_WIDTH).astype(BF16)

    u = jax.nn.gelu(proj(3 * ATTN_WIDTH, GMLP_WIDTH))
    vf = jax.nn.gelu(proj(3 * ATTN_WIDTH + GMLP_WIDTH, GMLP_WIDTH))
    mu = jnp.mean(vf, axis=-1, keepdims=True)
    vc = vf - mu
    var = jnp.mean(vc * vc, axis=-1, keepdims=True)
    vn = (vc * lax.rsqrt(var + EPS) * lng_ref[...] + lnb_ref[...]).astype(BF16)
    for c in range(tm // CHUNK):
        r = slice(c * CHUNK, (c + 1) * CHUNK)
        for g in range(N_GROUPS):
            cg = slice(g * CHUNK, (g + 1) * CHUNK)
            mixed = jnp.dot(ws_ref[g], vn[r, cg], preferred_element_type=F32) + bs_ref[g]
            gm_ref[r, cg] = (u[r, cg] * mixed).astype(BF16)


def _inproj(x, g, w, cos, sa, sb, lng, lnb, ws, bsb, seq):
    T = x.shape[0]
    tm = _tile(seq, 512)
    ns = seq // tm
    full = lambda i: (0, 0)
    out = jax.ShapeDtypeStruct((T, ATTN_WIDTH), BF16)
    return pl.pallas_call(
        _inproj_kernel,
        grid=(T // tm,),
        in_specs=[
            pl.BlockSpec((tm, D_MODEL), lambda i: (i, 0)),
            pl.BlockSpec((1, D_MODEL), full),
            pl.BlockSpec(w.shape, full),
            pl.BlockSpec((tm, HEAD_WIDTH), lambda i: (i % ns, 0)),
            pl.BlockSpec((tm, HEAD_WIDTH), lambda i: (i % ns, 0)),
            pl.BlockSpec((tm, HEAD_WIDTH), lambda i: (i % ns, 0)),
            pl.BlockSpec((1, GMLP_WIDTH), full),
            pl.BlockSpec((1, GMLP_WIDTH), full),
            pl.BlockSpec(ws.shape, lambda i: (0, 0, 0)),
            pl.BlockSpec(bsb.shape, lambda i: (0, 0, 0)),
        ],
        out_specs=[pl.BlockSpec((tm, ATTN_WIDTH), lambda i: (i, 0))] * 4,
        out_shape=[out] * 4,
        compiler_params=_params(("parallel",)),
        name="inproj",
    )(x, g, w, cos, sa, sb, lng, lnb, ws, bsb)


def _attn_kernel(q_ref, k_ref, v_ref, lam_ref, sub_ref, o_ref,
                 qcat_ref, sa_ref, sb_ref, m_ref, l_ref, acc_ref, *, tk, lam_init):
    tq = q_ref.shape[0]
    n = k_ref.shape[0] // tk
    qt = q_ref[...].astype(F32).T
    row = lax.broadcasted_iota(jnp.int32, qt.shape, 0)
    qcat_ref[:, :tq] = jnp.where(row < HEAD_DIM, qt, 0.0).astype(BF16)
    qcat_ref[:, tq:] = jnp.where(row >= HEAD_DIM, qt, 0.0).astype(BF16)
    m_ref[...] = jnp.full(m_ref.shape, -jnp.inf, F32)
    l_ref[...] = jnp.zeros(l_ref.shape, F32)
    acc_ref[...] = jnp.zeros(acc_ref.shape, F32)

    def rows(t):
        return pl.ds(pl.multiple_of(t * tk, tk), tk)

    def scores(t, s_ref):
        s_ref[...] = jnp.dot(k_ref[rows(t), :], qcat_ref[...], preferred_element_type=F32)

    def update(t, s_ref):
        vt = v_ref[rows(t), :]
        for cb in range(2 * tq // ATTN_COLS):
            cols = slice(cb * ATTN_COLS, (cb + 1) * ATTN_COLS)
            s = s_ref[:, cols]
            m_old = m_ref[:, cols]
            m_new = jnp.maximum(m_old, jnp.max(s, axis=0, keepdims=True))
            alpha = jnp.exp2(m_old - m_new)
            p = jnp.exp2(s - m_new)
            l_ref[:, cols] = alpha * l_ref[:, cols] + jnp.sum(p, axis=0, keepdims=True)
            m_ref[:, cols] = m_new
            pv = lax.dot_general(vt, p.astype(BF16), (((0,), (0,)), ((), ())),
                                 preferred_element_type=F32)
            acc_ref[:, cols] = alpha * acc_ref[:, cols] + pv

    scores(0, sa_ref)

    def pair(u, carry):
        t = 2 * u
        scores(t + 1, sb_ref)
        update(t, sa_ref)
        scores(t + 2, sa_ref)
        update(t + 1, sb_ref)
        return carry

    lax.fori_loop(0, n // 2 - 1, pair, 0)
    scores(n - 1, sb_ref)
    update(n - 2, sa_ref)
    update(n - 1, sb_ref)

    lp = lam_ref[...]
    lam = (jnp.exp(jnp.sum(lp[0:1] * lp[1:2], axis=-1, keepdims=True))
           - jnp.exp(jnp.sum(lp[2:3] * lp[3:4], axis=-1, keepdims=True)) + lam_init)
    l = l_ref[...]
    ot = acc_ref[:, :tq] / l[:, :tq] - lam * (acc_ref[:, tq:] / l[:, tq:])
    o = ot.T
    o_ref[...] = (_rms(o) * sub_ref[...] * (1.0 - lam_init)).astype(o_ref.dtype)


def _attention(q, k, v, lam_p, sub, batch, seq, lam_init):
    T = q.shape[0]
    tq = _tile(seq, ATTN_TQ)
    tk = _tile(seq // 2, ATTN_TK)
    nq = seq // tq
    kv_spec = pl.BlockSpec((seq, HEAD_WIDTH), lambda b, h, i: (b, h))
    return pl.pallas_call(
        functools.partial(_attn_kernel, tk=tk, lam_init=lam_init),
        grid=(batch, N_HEADS, nq),
        in_specs=[
            pl.BlockSpec((tq, HEAD_WIDTH), lambda b, h, i: (b * nq + i, h)),
            kv_spec,
            kv_spec,
            pl.BlockSpec(lam_p.shape, lambda b, h, i: (0, 0)),
            pl.BlockSpec((1, HEAD_WIDTH), lambda b, h, i: (0, 0)),
        ],
        out_specs=pl.BlockSpec((tq, HEAD_WIDTH), lambda b, h, i: (b * nq + i, h)),
        out_shape=jax.ShapeDtypeStruct((T, ATTN_WIDTH), BF16),
        scratch_shapes=[
            pltpu.VMEM((HEAD_WIDTH, 2 * tq), BF16),
            pltpu.VMEM((tk, 2 * tq), F32),
            pltpu.VMEM((tk, 2 * tq), F32),
            pltpu.VMEM((1, 2 * tq), F32),
            pltpu.VMEM((1, 2 * tq), F32),
            pltpu.VMEM((HEAD_WIDTH, 2 * tq), F32),
        ],
        compiler_params=_params(("parallel", "parallel", "parallel")),
        name="attn",
    )(q, k, v, lam_p, sub)


def _pack_bf16_pairs(h):
    half = D_MODEL // 2
    h16 = h.astype(BF16)
    lo = lax.bitcast_convert_type(h16[:, :half].astype(F32), jnp.uint32)
    hi = lax.bitcast_convert_type(h16[:, half:].astype(F32), jnp.uint32)
    return (lo >> 16) | (hi & jnp.uint32(0xFFFF0000))


def _unpack_bf16_pairs(xp):
    lo = lax.bitcast_convert_type(xp << 16, F32)
    hi = lax.bitcast_convert_type(xp & jnp.uint32(0xFFFF0000), F32)
    return lo.astype(BF16), hi.astype(BF16)


def _outproj_kernel(att_ref, gm_ref, x_ref, w_ref, g_ref, *rest, moe):
    if moe:
        r_ref, x1_ref, h_ref, route_ref, count_ref, cnt_ref = rest
    else:
        x1_ref, h_ref = rest
    y = (jnp.dot(att_ref[...], w_ref[:ATTN_WIDTH, :], preferred_element_type=F32)
         + jnp.dot(gm_ref[...], w_ref[ATTN_WIDTH:, :], preferred_element_type=F32))
    x1 = x_ref[...] + y
    x1_ref[...] = x1
    h = _rms(x1) * g_ref[...]
    if not moe:
        h_ref[...] = h.astype(BF16)
        return
    h_ref[...] = _pack_bf16_pairs(h)
    tm = h.shape[0]
    h_hi = h.astype(BF16)
    h_lo = (h - h_hi.astype(F32)).astype(BF16)
    part = (jnp.dot(h_hi, r_ref[...], preferred_element_type=F32)
            + jnp.dot(h_lo, r_ref[...], preferred_element_type=F32))
    logits = part + pltpu.roll(part, LANES - N_EXPERTS, 1)
    lane = lax.broadcasted_iota(jnp.int32, logits.shape, 1)
    lg = jnp.where(lane < N_EXPERTS, logits, -jnp.inf)
    v1 = jnp.max(lg, axis=-1, keepdims=True)
    i1 = jnp.min(jnp.where(lg == v1, lane, LANES), axis=-1, keepdims=True)
    lg2 = jnp.where(lane == i1, -jnp.inf, lg)
    v2 = jnp.max(lg2, axis=-1, keepdims=True)
    i2 = jnp.min(jnp.where(lg2 == v2, lane, LANES), axis=-1, keepdims=True)
    e2 = jnp.exp(v2 - v1)
    den = 1.0 + e2

    @pl.when(pl.program_id(0) == 0)
    def _():
        cnt_ref[...] = jnp.zeros(cnt_ref.shape, F32)

    sel1 = lane == i1
    sel2 = lane == i2
    onehot = jnp.where(jnp.logical_or(sel1, sel2), 1.0, 0.0)
    r_i = lax.broadcasted_iota(jnp.int32, (tm, tm), 0)
    c_i = lax.broadcasted_iota(jnp.int32, (tm, tm), 1)
    tri = jnp.where(c_i < r_i, 1.0, 0.0).astype(BF16)
    before = jnp.dot(tri, onehot.astype(BF16), preferred_element_type=F32) + cnt_ref[...]
    rank1 = jnp.sum(jnp.where(sel1, before, 0.0), axis=-1, keepdims=True)
    rank2 = jnp.sum(jnp.where(sel2, before, 0.0), axis=-1, keepdims=True)
    cnt = cnt_ref[...] + jnp.sum(onehot, axis=0, keepdims=True)
    cnt_ref[...] = cnt
    count_ref[...] = jnp.broadcast_to(cnt, count_ref.shape)
    route = jnp.zeros(logits.shape, F32)
    for col, val in enumerate((i1.astype(F32), i2.astype(F32), 1.0 / den, e2 / den, rank1, rank2)):
        route = jnp.where(lane == col, val, route)
    route_ref[...] = route


def _outproj(att, gm, x, w, g, router_pad):
    T = x.shape[0]
    tm = _tile(T, 512)
    moe = router_pad is not None
    full = lambda i: (0, 0)
    row = lambda i: (i, 0)
    in_specs = [
        pl.BlockSpec((tm, ATTN_WIDTH), row),
        pl.BlockSpec((tm, GMLP_WIDTH), row),
        pl.BlockSpec((tm, D_MODEL), row),
        pl.BlockSpec(w.shape, full),
        pl.BlockSpec((1, D_MODEL), full),
    ]
    args = [att, gm, x, w, g]
    if not moe:
        out_specs = [pl.BlockSpec((tm, D_MODEL), row), pl.BlockSpec((tm, D_MODEL), row)]
        out_shape = [jax.ShapeDtypeStruct((T, D_MODEL), F32),
                     jax.ShapeDtypeStruct((T, D_MODEL), BF16)]
        scratch = []
    else:
        in_specs.append(pl.BlockSpec(router_pad.shape, full))
        args.append(router_pad)
        out_specs = [pl.BlockSpec((tm, D_MODEL), row), pl.BlockSpec((tm, D_MODEL // 2), row),
                     pl.BlockSpec((tm, LANES), row), pl.BlockSpec((8, LANES), full)]
        out_shape = [jax.ShapeDtypeStruct((T, D_MODEL), F32),
                     jax.ShapeDtypeStruct((T, D_MODEL // 2), jnp.uint32),
                     jax.ShapeDtypeStruct((T, LANES), F32),
                     jax.ShapeDtypeStruct((8, LANES), F32)]
        scratch = [pltpu.VMEM((1, LANES), F32)]
    return pl.pallas_call(
        functools.partial(_outproj_kernel, moe=moe),
        grid=(T // tm,),
        in_specs=in_specs,
        out_specs=out_specs,
        out_shape=out_shape,
        scratch_shapes=scratch,
        compiler_params=_params(("arbitrary",) if moe else ("parallel",)),
        name="outproj_moe" if moe else "outproj",
    )(*args)


def _dispatch_kernel(dst_ref, h_ref, xs_in_ref, xs_ref, sem):
    del xs_in_ref
    tg = h_ref.shape[0]

    def row_copy(r, k):
        return pltpu.make_async_copy(h_ref.at[pl.ds(r, 1), :],
                                     xs_ref.at[pl.ds(dst_ref[TOP_K * r + k], 1), :], sem)

    def issue(r, carry):
        for k in range(TOP_K):
            row_copy(r, k).start(priority=k)
        return carry

    lax.fori_loop(0, tg, issue, 0, unroll=8)
    for k in range(TOP_K):
        pltpu.make_async_copy(h_ref, xs_ref.at[pl.ds(0, tg), :], sem).wait()


def _moe_dispatch(hp, dst, n_rows):
    T = hp.shape[0]
    tg = _tile(T, 512)
    xs0 = jnp.zeros((n_rows, D_MODEL // 2), jnp.uint32)
    return pl.pallas_call(
        _dispatch_kernel,
        grid=(T // tg,),
        in_specs=[
            pl.BlockSpec((TOP_K * tg,), lambda i: (i,), memory_space=pltpu.SMEM),
            pl.BlockSpec((tg, D_MODEL // 2), lambda i: (i, 0)),
            pl.BlockSpec(memory_space=pl.ANY),
        ],
        out_specs=pl.BlockSpec(memory_space=pl.ANY),
        out_shape=jax.ShapeDtypeStruct(xs0.shape, xs0.dtype),
        scratch_shapes=[pltpu.SemaphoreType.DMA(())],
        input_output_aliases={2: 0},
        compiler_params=_params(("arbitrary",)),
        name="moe_dispatch",
    )(dst, hp, xs0)


def _swiglu_step(h_ref, wg_ref, wu_ref, wd_ref, acc_ref):
    tm = h_ref.shape[0]
    rb = tm // FFN_ROW_SPLIT

    @pl.when(pl.program_id(1) == 0)
    def _():
        acc_ref[...] = jnp.zeros(acc_ref.shape, F32)

    for s in range(FFN_ROW_SPLIT):
        rows = slice(s * rb, (s + 1) * rb)
        h = h_ref[rows, :]
        a = (jax.nn.silu(jnp.dot(h, wg_ref[...], preferred_element_type=F32))
             * jnp.dot(h, wu_ref[...], preferred_element_type=F32)).astype(BF16)
        acc_ref[rows, :] += jnp.dot(a, wd_ref[...], preferred_element_type=F32)


def _dense_ffn_kernel(h_ref, x_ref, wg_ref, wu_ref, wd_ref, o_ref, acc_ref):
    _swiglu_step(h_ref, wg_ref, wu_ref, wd_ref, acc_ref)

    @pl.when(pl.program_id(1) == pl.num_programs(1) - 1)
    def _():
        o_ref[...] = x_ref[...] + acc_ref[...]


def _dense_ffn(h, x, wg, wu, wd):
    T = h.shape[0]
    tm = _tile(T, FFN_TM)
    tf = _tile(D_FF, 512)
    row = lambda i, j: (i, 0)
    return pl.pallas_call(
        _dense_ffn_kernel,
        grid=(T // tm, D_FF // tf),
        in_specs=[
            pl.BlockSpec((tm, D_MODEL), row),
            pl.BlockSpec((tm, D_MODEL), row),
            pl.BlockSpec((D_MODEL, tf), lambda i, j: (0, j)),
            pl.BlockSpec((D_MODEL, tf), lambda i, j: (0, j)),
            pl.BlockSpec((tf, D_MODEL), lambda i, j: (j, 0)),
        ],
        out_specs=pl.BlockSpec((tm, D_MODEL), row),
        out_shape=jax.ShapeDtypeStruct((T, D_MODEL), F32),
        scratch_shapes=[pltpu.VMEM((tm, D_MODEL), F32)],
        compiler_params=_params(("parallel", "arbitrary")),
        name="dense_ffn",
    )(h, x, wg, wu, wd)


def _moe_ffn_kernel(te_ref, nu_ref, xp_ref, wg_ref, wu_ref, wd_ref, o_ref, acc_ref, h_ref):
    i = pl.program_id(0)
    last = pl.program_id(1) == pl.num_programs(1) - 1

    @pl.when(i < nu_ref[0])
    def _():
        @pl.when(pl.program_id(1) == 0)
        def _():
            lo, hi = _unpack_bf16_pairs(xp_ref[...])
            h_ref[:, :D_MODEL // 2] = lo
            h_ref[:, D_MODEL // 2:] = hi

        _swiglu_step(h_ref, wg_ref, wu_ref, wd_ref, acc_ref)

        @pl.when(last)
        def _():
            o_ref[...] = acc_ref[...]

    @pl.when(jnp.logical_and(i >= nu_ref[0], last))
    def _():
        o_ref[...] = jnp.zeros(o_ref.shape, o_ref.dtype)


def _moe_ffn(xs, tile_expert, n_used, wg, wu, wd, tm):
    P = xs.shape[0]
    tf = _tile(D_FF, 512)
    row = lambda i, j, te, nu: (i, 0)
    return pl.pallas_call(
        _moe_ffn_kernel,
        grid_spec=pltpu.PrefetchScalarGridSpec(
            num_scalar_prefetch=2,
            grid=(P // tm, D_FF // tf),
            in_specs=[
                pl.BlockSpec((tm, D_MODEL // 2), row),
                pl.BlockSpec((None, D_MODEL, tf), lambda i, j, te, nu: (te[i], 0, j)),
                pl.BlockSpec((None, D_MODEL, tf), lambda i, j, te, nu: (te[i], 0, j)),
                pl.BlockSpec((None, tf, D_MODEL), lambda i, j, te, nu: (te[i], j, 0)),
            ],
            out_specs=pl.BlockSpec((tm, D_MODEL), row),
            scratch_shapes=[pltpu.VMEM((tm, D_MODEL), F32), pltpu.VMEM((tm, D_MODEL), BF16)],
        ),
        out_shape=jax.ShapeDtypeStruct((P, D_MODEL), F32),
        compiler_params=_params(("arbitrary", "arbitrary")),
        name="moe_ffn",
    )(tile_expert, n_used, xs, wg, wu, wd)


def _ple_kernel(*refs, moe, last):
    refs = list(refs)
    x_ref = refs.pop(0)
    if moe:
        dst_ref, ys_ref, route_ref = refs.pop(0), refs.pop(0), refs.pop(0)
    p_ref, wg_ref, b_ref, pw_ref, pn_ref = refs[:5]
    refs = refs[5:]
    if last:
        fn_ref = refs.pop(0)
    o_ref = refs.pop(0)
    tm = x_ref.shape[0]

    if moe:
        ybuf_ref, sem = refs

        def row_copy(r, k):
            return pltpu.make_async_copy(ys_ref.at[pl.ds(dst_ref[TOP_K * r + k], 1), :],
                                         ybuf_ref.at[k, pl.ds(r, 1), :], sem)

        def issue(r, carry):
            for k in range(TOP_K):
                row_copy(r, k).start(priority=k)
            return carry

        lax.fori_loop(0, tm, issue, 0, unroll=8)

    e = jnp.dot(p_ref[...].astype(BF16), pw_ref[...], preferred_element_type=F32)
    e = _rms(e) * pn_ref[...]
    x = x_ref[...]
    if moe:
        for k in range(TOP_K):
            pltpu.make_async_copy(ys_ref.at[pl.ds(0, tm), :], ybuf_ref.at[k], sem).wait()
        rt = route_ref[...]
        x = x + (rt[:, 2:3] * ybuf_ref[0] + rt[:, 3:4] * ybuf_ref[1])
    gate = jax.nn.sigmoid(jnp.dot(_rms(x).astype(BF16), wg_ref[...], preferred_element_type=F32)
                          + b_ref[...])
    x = x + gate * e
    if last:
        x = _rms(x) * fn_ref[...]
    o_ref[...] = x


def _ple(x, dst, ys, route, p, layer, wg, b, pw, pn, final_norm):
    T = x.shape[0]
    tm = _tile(T, 512)
    moe = ys is not None
    last = final_norm is not None
    full = lambda i: (0, 0)
    row = lambda i: (i, 0)
    in_specs = [pl.BlockSpec((tm, D_MODEL), row)]
    args = [x]
    scratch = []
    if moe:
        in_specs += [pl.BlockSpec((TOP_K * tm,), lambda i: (i,), memory_space=pltpu.SMEM),
                     pl.BlockSpec(memory_space=pl.ANY),
                     pl.BlockSpec((tm, LANES), row)]
        args += [dst, ys, route]
        scratch = [pltpu.VMEM((TOP_K, tm, D_MODEL), F32), pltpu.SemaphoreType.DMA(())]
    in_specs += [
        pl.BlockSpec((None, tm, PLE_DIM), lambda i: (layer, i, 0)),
        pl.BlockSpec(wg.shape, full),
        pl.BlockSpec((1, D_MODEL), full),
        pl.BlockSpec(pw.shape, full),
        pl.BlockSpec((1, D_MODEL), full),
    ]
    args += [p, wg, b, pw, pn]
    if last:
        in_specs.append(pl.BlockSpec((1, D_MODEL), full))
        args.append(final_norm)
    return pl.pallas_call(
        functools.partial(_ple_kernel, moe=moe, last=last),
        grid=(T // tm,),
        in_specs=in_specs,
        out_specs=pl.BlockSpec((tm, D_MODEL), row),
        out_shape=jax.ShapeDtypeStruct((T, D_MODEL), F32),
        scratch_shapes=scratch,
        compiler_params=_params(("parallel",)),
        name="ple" + ("_moe" if moe else "") + ("_last" if last else ""),
    )(*args)


def _rope_tables(seq):
    half = ROT_DIM // 2
    pos = jnp.arange(seq, dtype=F32)
    inv = ROPE_THETA ** (-jnp.arange(half, dtype=F32) * 2.0 / ROT_DIM)
    ang = pos[:, None] * inv[None, :]
    cos, sin = jnp.cos(ang), jnp.sin(ang)
    ones = jnp.ones((seq, HEAD_DIM - ROT_DIM), F32)
    zeros = jnp.zeros((seq, HEAD_DIM - ROT_DIM), F32)
    zh = jnp.zeros((seq, half), F32)
    c = jnp.concatenate([cos, cos, ones], axis=1)
    sa = jnp.concatenate([zh, sin, zeros], axis=1)
    sb = jnp.concatenate([-sin, zh, zeros], axis=1)
    rep = lambda t: jnp.concatenate([t, t], axis=1)
    return rep(c), rep(sa), rep(sb)


def _group_layout(route, counts, tm):
    T = route.shape[0]
    counts = counts[0, :N_EXPERTS].astype(jnp.int32)
    n_tile_e = (counts + tm - 1) // tm
    tile_end = jnp.cumsum(n_tile_e)
    start = (tile_end - n_tile_e) * tm
    expert = route[:, 0:TOP_K].astype(jnp.int32)
    rank = route[:, 4:4 + TOP_K].astype(jnp.int32)
    onehot = expert[..., None] == jnp.arange(N_EXPERTS, dtype=jnp.int32)
    dst = (jnp.sum(jnp.where(onehot, start, 0), axis=-1) + rank).reshape(-1)
    n_tiles = TOP_K * T // tm + N_EXPERTS
    tile_expert = jnp.minimum(
        jnp.sum((jnp.arange(n_tiles, dtype=jnp.int32)[:, None] >= tile_end[None, :])
                .astype(jnp.int32), axis=1), N_EXPERTS - 1)
    return dst, tile_expert, tile_end[-1:], n_tiles * tm


def _trunk(x, p, W):
    B, S, _ = x.shape
    T = B * S
    depth = p.shape[0]
    x = x.reshape(T, D_MODEL)
    p = p.reshape(depth, T, PLE_DIM)
    cos, sa, sb = _rope_tables(S)
    for i in range(depth):
        lam_init = 0.8 - 0.6 * math.exp(-0.3 * i)
        q, k, v, gm = _inproj(x, W["norm_mix"][i], W["w_in"][i], cos, sa, sb, W["gmlp_ln_g"][i],
                              W["gmlp_ln_b"][i], W["gmlp_ws"][i], W["gmlp_bs"][i], S)
        att = _attention(q, k, v, W["lam"][i], W["subln"][i], B, S, lam_init)
        moe = i % 2 == 1
        j = i // 2
        fin = W["final_norm"] if i == depth - 1 else None
        ple_w = (W["ple_gate_w"][i], W["ple_gate_b"][i], W["ple_w"][i], W["ple_norm"][i], fin)
        if not moe:
            x1, h2 = _outproj(att, gm, x, W["w_out"][i], W["norm_ffn"][i], None)
            x2 = _dense_ffn(h2, x1, W["dense_w_gate"][j], W["dense_w_up"][j], W["dense_w_down"][j])
            x = _ple(x2, None, None, None, p, i, *ple_w)
        else:
            x1, hp, route, counts = _outproj(att, gm, x, W["w_out"][i], W["norm_ffn"][i],
                                             W["router"][j])
            tm = _tile(T, MOE_TM)
            dst, tile_expert, n_used, n_rows = _group_layout(route, counts, tm)
            xs = _moe_dispatch(hp, dst, n_rows)
            ys = _moe_ffn(xs, tile_expert, n_used, W["moe_w_gate"][j], W["moe_w_up"][j],
                          W["moe_w_down"][j], tm)
            x = _ple(x1, dst, ys, route, p, i, *ple_w)
    return x.reshape(B, S, D_MODEL)


def kernel(x_prompt, x_sample, p_prompt, p_sample, norm_mix, w_in, lambda_q1, lambda_k1, lambda_q2, lambda_k2, subln, gmlp_ln_g, gmlp_ln_b, gmlp_ws, gmlp_bs, w_out, norm_ffn, dense_w_gate, dense_w_up, dense_w_down, router, moe_w_gate, moe_w_up, moe_w_down, ple_w, ple_norm, ple_gate_w, ple_gate_b, final_norm):
    depth = w_in.shape[0]
    row = lambda t: t[:, None, :]
    lam = jnp.stack([lambda_q1, lambda_k1, lambda_q2, lambda_k2], axis=1)
    lam = jnp.pad(lam, ((0, 0), (0, 4), (0, LANES - HEAD_DIM)))
    r_hi = router.astype(BF16)
    r_lo = (router - r_hi.astype(F32)).astype(BF16)
    W = dict(
        norm_mix=row(norm_mix), w_in=w_in.astype(BF16), lam=lam, subln=row(subln),
        gmlp_ln_g=row(gmlp_ln_g), gmlp_ln_b=row(gmlp_ln_b), gmlp_ws=gmlp_ws.astype(BF16),
        gmlp_bs=jnp.broadcast_to(gmlp_bs[..., None], gmlp_bs.shape + (CHUNK,)),
        w_out=w_out.astype(BF16), norm_ffn=row(norm_ffn),
        dense_w_gate=dense_w_gate.astype(BF16), dense_w_up=dense_w_up.astype(BF16),
        dense_w_down=dense_w_down.astype(BF16),
        router=jnp.pad(jnp.concatenate([r_hi, r_lo], axis=-1),
                       ((0, 0), (0, 0), (0, LANES - 2 * N_EXPERTS))),
        moe_w_gate=moe_w_gate.astype(BF16), moe_w_up=moe_w_up.astype(BF16),
        moe_w_down=moe_w_down.astype(BF16),
        ple_w=ple_w.astype(BF16), ple_norm=row(ple_norm), ple_gate_w=ple_gate_w.astype(BF16),
        ple_gate_b=row(ple_gate_b), final_norm=final_norm[None, :],
    )
    return (_trunk(x_prompt, p_prompt, W), _trunk(x_sample, p_sample, W))
```

```python
import functools
import math

import jax
import jax.numpy as jnp
from jax import lax
from jax.experimental import pallas as pl
from jax.experimental.pallas import tpu as pltpu

F32 = jnp.float32
BF16 = jnp.bfloat16

D_MODEL = 1024
ATTN_WIDTH = 512
GMLP_WIDTH = 512
HEAD_DIM = 64
HEAD_WIDTH = 2 * HEAD_DIM
N_HEADS = ATTN_WIDTH // HEAD_WIDTH
ROT_DIM = HEAD_DIM // 4
ROPE_THETA = 500000.0
CHUNK = 128
N_GROUPS = 4
D_FF = 3584
N_EXPERTS = 8
TOP_K = 2
PLE_DIM = 256
EPS = 1e-6
LANES = 128
Q_SCALE = (1.0 / math.sqrt(HEAD_DIM)) * math.log2(math.e)
VMEM_LIMIT = 48 * 1024 * 1024
ATTN_TQ = 1024
ATTN_TK = 512
ATTN_COLS = 256
FFN_TM = 512
MOE_TM = 512
FFN_ROW_SPLIT = 2
ROW_TM = 1024
ROW_SPLIT = 1


def _tile(n, pref):
    t = min(n, pref)
    while n % t:
        t //= 2
    return t


def _params(sem, **kw):
    return pltpu.CompilerParams(dimension_semantics=sem, vmem_limit_bytes=VMEM_LIMIT, **kw)


def _rms(x):
    return x * lax.rsqrt(jnp.mean(x * x, axis=-1, keepdims=True) + EPS)


def _inproj_kernel(x_ref, g_ref, w_ref, cos_ref, sa_ref, sb_ref, lng_ref, lnb_ref, ws_ref, bs_ref,
                   q_ref, k_ref, v_ref, gm_ref):
    tm = x_ref.shape[0]
    rb = tm // ROW_SPLIT
    half = ROT_DIM // 2
    for s in range(ROW_SPLIT):
        rows = slice(s * rb, (s + 1) * rb)
        h = (_rms(x_ref[rows, :]) * g_ref[...]).astype(BF16)

        def proj(c0, n):
            return jnp.dot(h, w_ref[:, c0:c0 + n], preferred_element_type=F32)

        cos, sa, sb = cos_ref[rows, :], sa_ref[rows, :], sb_ref[rows, :]

        def rope(t):
            outs = []
            for hh in range(N_HEADS):
                th = t[:, hh * HEAD_WIDTH:(hh + 1) * HEAD_WIDTH]
                outs.append(th * cos + pltpu.roll(th, half, 1) * sa
                            + pltpu.roll(th, HEAD_WIDTH - half, 1) * sb)
            return jnp.concatenate(outs, axis=1)

        q_ref[rows, :] = (rope(proj(0, ATTN_WIDTH)) * Q_SCALE).astype(BF16)
        k_ref[rows, :] = rope(proj(ATTN_WIDTH, ATTN_WIDTH)).astype(BF16)
        v_ref[rows, :] = proj(2 * ATTN_WIDTH, ATTN_WIDTH).astype(BF16)

        u = jax.nn.gelu(proj(3 * ATTN_WIDTH, GMLP_WIDTH))
        vf = jax.nn.gelu(proj(3 * ATTN_WIDTH + GMLP_WIDTH, GMLP_WIDTH))
        mu = jnp.mean(vf, axis=-1, keepdims=True)
        vc = vf - mu
        var = jnp.mean(vc * vc, axis=-1, keepdims=True)
        vn = (vc * lax.rsqrt(var + EPS) * lng_ref[...] + lnb_ref[...]).astype(BF16)
        for c in range(rb // CHUNK):
            r = slice(c * CHUNK, (c + 1) * CHUNK)
            ro = slice(s * rb + c * CHUNK, s * rb + (c + 1) * CHUNK)
            for g in range(N_GROUPS):
                cg = slice(g * CHUNK, (g + 1) * CHUNK)
                mixed = jnp.dot(ws_ref[g], vn[r, cg], preferred_element_type=F32) + bs_ref[g]
                gm_ref[ro, cg] = (u[r, cg] * mixed).astype(BF16)


def _inproj(x, g, w, cos, sa, sb, lng, lnb, ws, bsb, seq):
    T = x.shape[0]
    tm = _tile(seq, ROW_TM)
    ns = seq // tm
    full = lambda i: (0, 0)
    out = jax.ShapeDtypeStruct((T, ATTN_WIDTH), BF16)
    return pl.pallas_call(
        _inproj_kernel,
        grid=(T // tm,),
        in_specs=[
            pl.BlockSpec((tm, D_MODEL), lambda i: (i, 0)),
            pl.BlockSpec((1, D_MODEL), full),
            pl.BlockSpec(w.shape, full),
            pl.BlockSpec((tm, HEAD_WIDTH), lambda i: (i % ns, 0)),
            pl.BlockSpec((tm, HEAD_WIDTH), lambda i: (i % ns, 0)),
            pl.BlockSpec((tm, HEAD_WIDTH), lambda i: (i % ns, 0)),
            pl.BlockSpec((1, GMLP_WIDTH), full),
            pl.BlockSpec((1, GMLP_WIDTH), full),
            pl.BlockSpec(ws.shape, lambda i: (0, 0, 0)),
            pl.BlockSpec(bsb.shape, lambda i: (0, 0, 0)),
        ],
        out_specs=[pl.BlockSpec((tm, ATTN_WIDTH), lambda i: (i, 0))] * 4,
        out_shape=[out] * 4,
        compiler_params=_params(("parallel",)),
        name="inproj",
    )(x, g, w, cos, sa, sb, lng, lnb, ws, bsb)


def _attn_kernel(q_ref, k_ref, v_ref, lam_ref, sub_ref, o_ref,
                 qcat_ref, sa_ref, sb_ref, m_ref, l_ref, acc_ref, *, tk, lam_init):
    tq = q_ref.shape[0]
    n = k_ref.shape[0] // tk
    qt = q_ref[...].astype(F32).T
    row = lax.broadcasted_iota(jnp.int32, qt.shape, 0)
    qcat_ref[:, :tq] = jnp.where(row < HEAD_DIM, qt, 0.0).astype(BF16)
    qcat_ref[:, tq:] = jnp.where(row >= HEAD_DIM, qt, 0.0).astype(BF16)
    m_ref[...] = jnp.full(m_ref.shape, -jnp.inf, F32)
    l_ref[...] = jnp.zeros(l_ref.shape, F32)
    acc_ref[...] = jnp.zeros(acc_ref.shape, F32)

    def rows(t):
        return pl.ds(pl.multiple_of(t * tk, tk), tk)

    def scores(t, s_ref):
        s_ref[...] = jnp.dot(k_ref[rows(t), :], qcat_ref[...], preferred_element_type=F32)

    def update(t, s_ref):
        vt = v_ref[rows(t), :]
        for cb in range(2 * tq // ATTN_COLS):
            cols = slice(cb * ATTN_COLS, (cb + 1) * ATTN_COLS)
            s = s_ref[:, cols]
            m_old = m_ref[:, cols]
            m_new = jnp.maximum(m_old, jnp.max(s, axis=0, keepdims=True))
            alpha = jnp.exp2(m_old - m_new)
            p = jnp.exp2(s - m_new)
            l_ref[:, cols] = alpha * l_ref[:, cols] + jnp.sum(p, axis=0, keepdims=True)
            m_ref[:, cols] = m_new
            pv = lax.dot_general(vt, p.astype(BF16), (((0,), (0,)), ((), ())),
                                 preferred_element_type=F32)
            acc_ref[:, cols] = alpha * acc_ref[:, cols] + pv

    scores(0, sa_ref)

    def pair(u, carry):
        t = 2 * u
        scores(t + 1, sb_ref)
        update(t, sa_ref)
        scores(t + 2, sa_ref)
        update(t + 1, sb_ref)
        return carry

    lax.fori_loop(0, n // 2 - 1, pair, 0)
    scores(n - 1, sb_ref)
    update(n - 2, sa_ref)
    update(n - 1, sb_ref)

    lp = lam_ref[...]
    lam = (jnp.exp(jnp.sum(lp[0:1] * lp[1:2], axis=-1, keepdims=True))
           - jnp.exp(jnp.sum(lp[2:3] * lp[3:4], axis=-1, keepdims=True)) + lam_init)
    l = l_ref[...]
    ot = acc_ref[:, :tq] / l[:, :tq] - lam * (acc_ref[:, tq:] / l[:, tq:])
    o = ot.T
    o_ref[...] = (_rms(o) * sub_ref[...] * (1.0 - lam_init)).astype(o_ref.dtype)


def _attention(q, k, v, lam_p, sub, batch, seq, lam_init):
    T = q.shape[0]
    tq = _tile(seq, ATTN_TQ)
    tk = _tile(seq // 2, ATTN_TK)
    nq = seq // tq
    kv_spec = pl.BlockSpec((seq, HEAD_WIDTH), lambda b, h, i: (b, h))
    return pl.pallas_call(
        functools.partial(_attn_kernel, tk=tk, lam_init=lam_init),
        grid=(batch, N_HEADS, nq),
        in_specs=[
            pl.BlockSpec((tq, HEAD_WIDTH), lambda b, h, i: (b * nq + i, h)),
            kv_spec,
            kv_spec,
            pl.BlockSpec(lam_p.shape, lambda b, h, i: (0, 0)),
            pl.BlockSpec((1, HEAD_WIDTH), lambda b, h, i: (0, 0)),
        ],
        out_specs=pl.BlockSpec((tq, HEAD_WIDTH), lambda b, h, i: (b * nq + i, h)),
        out_shape=jax.ShapeDtypeStruct((T, ATTN_WIDTH), BF16),
        scratch_shapes=[
            pltpu.VMEM((HEAD_WIDTH, 2 * tq), BF16),
            pltpu.VMEM((tk, 2 * tq), F32),
            pltpu.VMEM((tk, 2 * tq), F32),
            pltpu.VMEM((1, 2 * tq), F32),
            pltpu.VMEM((1, 2 * tq), F32),
            pltpu.VMEM((HEAD_WIDTH, 2 * tq), F32),
        ],
        compiler_params=_params(("parallel", "parallel", "parallel")),
        name="attn",
    )(q, k, v, lam_p, sub)


def _pack_bf16_pairs(h):
    half = D_MODEL // 2
    h16 = h.astype(BF16)
    lo = lax.bitcast_convert_type(h16[:, :half].astype(F32), jnp.uint32)
    hi = lax.bitcast_convert_type(h16[:, half:].astype(F32), jnp.uint32)
    return (lo >> 16) | (hi & jnp.uint32(0xFFFF0000))


def _unpack_bf16_pairs(xp):
    lo = lax.bitcast_convert_type(xp << 16, F32)
    hi = lax.bitcast_convert_type(xp & jnp.uint32(0xFFFF0000), F32)
    return lo.astype(BF16), hi.astype(BF16)


def _outproj_kernel(att_ref, gm_ref, x_ref, w_ref, g_ref, *rest, moe):
    if moe:
        r_ref, x1_ref, h_ref, route_ref, count_ref, cnt_ref = rest
    else:
        x1_ref, h_ref = rest
    y = (jnp.dot(att_ref[...], w_ref[:ATTN_WIDTH, :], preferred_element_type=F32)
         + jnp.dot(gm_ref[...], w_ref[ATTN_WIDTH:, :], preferred_element_type=F32))
    x1 = x_ref[...] + y
    x1_ref[...] = x1
    h = _rms(x1) * g_ref[...]
    if not moe:
        h_ref[...] = h.astype(BF16)
        return
    h_ref[...] = _pack_bf16_pairs(h)
    tm = h.shape[0]
    h_hi = h.astype(BF16)
    h_lo = (h - h_hi.astype(F32)).astype(BF16)
    part = (jnp.dot(h_hi, r_ref[...], preferred_element_type=F32)
            + jnp.dot(h_lo, r_ref[...], preferred_element_type=F32))
    logits = part + pltpu.roll(part, LANES - N_EXPERTS, 1)
    lane = lax.broadcasted_iota(jnp.int32, logits.shape, 1)
    lg = jnp.where(lane < N_EXPERTS, logits, -jnp.inf)
    v1 = jnp.max(lg, axis=-1, keepdims=True)
    i1 = jnp.min(jnp.where(lg == v1, lane, LANES), axis=-1, keepdims=True)
    lg2 = jnp.where(lane == i1, -jnp.inf, lg)
    v2 = jnp.max(lg2, axis=-1, keepdims=True)
    i2 = jnp.min(jnp.where(lg2 == v2, lane, LANES), axis=-1, keepdims=True)
    e2 = jnp.exp(v2 - v1)
    den = 1.0 + e2

    @pl.when(pl.program_id(0) == 0)
    def _():
        cnt_ref[...] = jnp.zeros(cnt_ref.shape, F32)

    sel1 = lane == i1
    sel2 = lane == i2
    onehot = jnp.where(jnp.logical_or(sel1, sel2), 1.0, 0.0)
    r_i = lax.broadcasted_iota(jnp.int32, (tm, tm), 0)
    c_i = lax.broadcasted_iota(jnp.int32, (tm, tm), 1)
    tri = jnp.where(c_i < r_i, 1.0, 0.0).astype(BF16)
    before = jnp.dot(tri, onehot.astype(BF16), preferred_element_type=F32) + cnt_ref[...]
    rank1 = jnp.sum(jnp.where(sel1, before, 0.0), axis=-1, keepdims=True)
    rank2 = jnp.sum(jnp.where(sel2, before, 0.0), axis=-1, keepdims=True)
    cnt = cnt_ref[...] + jnp.sum(onehot, axis=0, keepdims=True)
    cnt_ref[...] = cnt
    count_ref[...] = jnp.broadcast_to(cnt, count_ref.shape)
    route = jnp.zeros(logits.shape, F32)
    for col, val in enumerate((i1.astype(F32), i2.astype(F32), 1.0 / den, e2 / den, rank1, rank2)):
        route = jnp.where(lane == col, val, route)
    route_ref[...] = route


def _outproj(att, gm, x, w, g, router_pad):
    T = x.shape[0]
    tm = _tile(T, 512)
    moe = router_pad is not None
    full = lambda i: (0, 0)
    row = lambda i: (i, 0)
    in_specs = [
        pl.BlockSpec((tm, ATTN_WIDTH), row),
        pl.BlockSpec((tm, GMLP_WIDTH), row),
        pl.BlockSpec((tm, D_MODEL), row),
        pl.BlockSpec(w.shape, full),
        pl.BlockSpec((1, D_MODEL), full),
    ]
    args = [att, gm, x, w, g]
    if not moe:
        out_specs = [pl.BlockSpec((tm, D_MODEL), row), pl.BlockSpec((tm, D_MODEL), row)]
        out_shape = [jax.ShapeDtypeStruct((T, D_MODEL), F32),
                     jax.ShapeDtypeStruct((T, D_MODEL), BF16)]
        scratch = []
    else:
        in_specs.append(pl.BlockSpec(router_pad.shape, full))
        args.append(router_pad)
        out_specs = [pl.BlockSpec((tm, D_MODEL), row), pl.BlockSpec((tm, D_MODEL // 2), row),
                     pl.BlockSpec((tm, LANES), row), pl.BlockSpec((8, LANES), full)]
        out_shape = [jax.ShapeDtypeStruct((T, D_MODEL), F32),
                     jax.ShapeDtypeStruct((T, D_MODEL // 2), jnp.uint32),
                     jax.ShapeDtypeStruct((T, LANES), F32),
                     jax.ShapeDtypeStruct((8, LANES), F32)]
        scratch = [pltpu.VMEM((1, LANES), F32)]
    return pl.pallas_call(
        functools.partial(_outproj_kernel, moe=moe),
        grid=(T // tm,),
        in_specs=in_specs,
        out_specs=out_specs,
        out_shape=out_shape,
        scratch_shapes=scratch,
        compiler_params=_params(("arbitrary",) if moe else ("parallel",)),
        name="outproj_moe" if moe else "outproj",
    )(*args)


def _dispatch_kernel(dst_ref, h_ref, xs_in_ref, xs_ref, sem):
    del xs_in_ref
    tg = h_ref.shape[0]

    def row_copy(r, k):
        return pltpu.make_async_copy(h_ref.at[pl.ds(r, 1), :],
                                     xs_ref.at[pl.ds(dst_ref[TOP_K * r + k], 1), :], sem)

    def issue(r, carry):
        for k in range(TOP_K):
            row_copy(r, k).start(priority=k)
        return carry

    lax.fori_loop(0, tg, issue, 0, unroll=8)
    for k in range(TOP_K):
        pltpu.make_async_copy(h_ref, xs_ref.at[pl.ds(0, tg), :], sem).wait()


def _moe_dispatch(hp, dst, n_rows):
    T = hp.shape[0]
    tg = _tile(T, 512)
    xs0 = jnp.zeros((n_rows, D_MODEL // 2), jnp.uint32)
    return pl.pallas_call(
        _dispatch_kernel,
        grid=(T // tg,),
        in_specs=[
            pl.BlockSpec((TOP_K * tg,), lambda i: (i,), memory_space=pltpu.SMEM),
            pl.BlockSpec((tg, D_MODEL // 2), lambda i: (i, 0)),
            pl.BlockSpec(memory_space=pl.ANY),
        ],
        out_specs=pl.BlockSpec(memory_space=pl.ANY),
        out_shape=jax.ShapeDtypeStruct(xs0.shape, xs0.dtype),
        scratch_shapes=[pltpu.SemaphoreType.DMA(())],
        input_output_aliases={2: 0},
        compiler_params=_params(("arbitrary",)),
        name="moe_dispatch",
    )(dst, hp, xs0)


def _swiglu_step(h_ref, wg_ref, wu_ref, wd_ref, acc_ref):
    tm = h_ref.shape[0]
    rb = tm // FFN_ROW_SPLIT

    @pl.when(pl.program_id(1) == 0)
    def _():
        acc_ref[...] = jnp.zeros(acc_ref.shape, F32)

    for s in range(FFN_ROW_SPLIT):
        rows = slice(s * rb, (s + 1) * rb)
        h = h_ref[rows, :]
        a = (jax.nn.silu(jnp.dot(h, wg_ref[...], preferred_element_type=F32))
             * jnp.dot(h, wu_ref[...], preferred_element_type=F32)).astype(BF16)
        acc_ref[rows, :] += jnp.dot(a, wd_ref[...], preferred_element_type=F32)


def _dense_ffn_kernel(h_ref, x_ref, wg_ref, wu_ref, wd_ref, o_ref, acc_ref):
    _swiglu_step(h_ref, wg_ref, wu_ref, wd_ref, acc_ref)

    @pl.when(pl.program_id(1) == pl.num_programs(1) - 1)
    def _():
        o_ref[...] = x_ref[...] + acc_ref[...]


def _dense_ffn(h, x, wg, wu, wd):
    T = h.shape[0]
    tm = _tile(T, FFN_TM)
    tf = D_FF // 2 if D_FF % 512 == 0 else _tile(D_FF, 512)
    row = lambda i, j: (i, 0)
    return pl.pallas_call(
        _dense_ffn_kernel,
        grid=(T // tm, D_FF // tf),
        in_specs=[
            pl.BlockSpec((tm, D_MODEL), row),
            pl.BlockSpec((tm, D_MODEL), row),
            pl.BlockSpec((D_MODEL, tf), lambda i, j: (0, j)),
            pl.BlockSpec((D_MODEL, tf), lambda i, j: (0, j)),
            pl.BlockSpec((tf, D_MODEL), lambda i, j: (j, 0)),
        ],
        out_specs=pl.BlockSpec((tm, D_MODEL), row),
        out_shape=jax.ShapeDtypeStruct((T, D_MODEL), F32),
        scratch_shapes=[pltpu.VMEM((tm, D_MODEL), F32)],
        compiler_params=_params(("parallel", "arbitrary")),
        name="dense_ffn",
    )(h, x, wg, wu, wd)


def _moe_ffn_kernel(te_ref, nu_ref, xp_ref, wg_ref, wu_ref, wd_ref, o_ref, acc_ref, h_ref):
    i = pl.program_id(0)
    last = pl.program_id(1) == pl.num_programs(1) - 1

    @pl.when(i < nu_ref[0])
    def _():
        @pl.when(pl.program_id(1) == 0)
        def _():
            lo, hi = _unpack_bf16_pairs(xp_ref[...])
            h_ref[:, :D_MODEL // 2] = lo
            h_ref[:, D_MODEL // 2:] = hi

        _swiglu_step(h_ref, wg_ref, wu_ref, wd_ref, acc_ref)

        @pl.when(last)
        def _():
            o_ref[...] = acc_ref[...]

    @pl.when(jnp.logical_and(i >= nu_ref[0], last))
    def _():
        o_ref[...] = jnp.zeros(o_ref.shape, o_ref.dtype)


def _moe_ffn(xs, tile_expert, n_used, wg, wu, wd, tm):
    P = xs.shape[0]
    tf = D_FF // 2 if D_FF % 512 == 0 else _tile(D_FF, 512)
    row = lambda i, j, te, nu: (i, 0)
    return pl.pallas_call(
        _moe_ffn_kernel,
        grid_spec=pltpu.PrefetchScalarGridSpec(
            num_scalar_prefetch=2,
            grid=(P // tm, D_FF // tf),
            in_specs=[
                pl.BlockSpec((tm, D_MODEL // 2), row),
                pl.BlockSpec((None, D_MODEL, tf), lambda i, j, te, nu: (te[i], 0, j)),
                pl.BlockSpec((None, D_MODEL, tf), lambda i, j, te, nu: (te[i], 0, j)),
                pl.BlockSpec((None, tf, D_MODEL), lambda i, j, te, nu: (te[i], j, 0)),
            ],
            out_specs=pl.BlockSpec((tm, D_MODEL), row),
            scratch_shapes=[pltpu.VMEM((tm, D_MODEL), F32), pltpu.VMEM((tm, D_MODEL), BF16)],
        ),
        out_shape=jax.ShapeDtypeStruct((P, D_MODEL), F32),
        compiler_params=_params(("arbitrary", "arbitrary")),
        name="moe_ffn",
    )(tile_expert, n_used, xs, wg, wu, wd)


def _ple_kernel(*refs, moe, last):
    refs = list(refs)
    x_ref = refs.pop(0)
    if moe:
        dst_ref, ys_ref, route_ref = refs.pop(0), refs.pop(0), refs.pop(0)
    p_ref, wg_ref, b_ref, pw_ref, pn_ref = refs[:5]
    refs = refs[5:]
    if last:
        fn_ref = refs.pop(0)
    o_ref = refs.pop(0)
    tm = x_ref.shape[0]

    if moe:
        ybuf_ref, sem = refs

        def row_copy(r, k):
            return pltpu.make_async_copy(ys_ref.at[pl.ds(dst_ref[TOP_K * r + k], 1), :],
                                         ybuf_ref.at[k, pl.ds(r, 1), :], sem)

        def issue(r, carry):
            for k in range(TOP_K):
                row_copy(r, k).start(priority=k)
            return carry

        lax.fori_loop(0, tm, issue, 0, unroll=8)

    e = jnp.dot(p_ref[...].astype(BF16), pw_ref[...], preferred_element_type=F32)
    e = _rms(e) * pn_ref[...]
    x = x_ref[...]
    if moe:
        for k in range(TOP_K):
            pltpu.make_async_copy(ys_ref.at[pl.ds(0, tm), :], ybuf_ref.at[k], sem).wait()
        rt = route_ref[...]
        x = x + (rt[:, 2:3] * ybuf_ref[0] + rt[:, 3:4] * ybuf_ref[1])
    gate = jax.nn.sigmoid(jnp.dot(_rms(x).astype(BF16), wg_ref[...], preferred_element_type=F32)
                          + b_ref[...])
    x = x + gate * e
    if last:
        x = _rms(x) * fn_ref[...]
    o_ref[...] = x


def _ple(x, dst, ys, route, p, layer, wg, b, pw, pn, final_norm):
    T = x.shape[0]
    tm = _tile(T, 512)
    moe = ys is not None
    last = final_norm is not None
    full = lambda i: (0, 0)
    row = lambda i: (i, 0)
    in_specs = [pl.BlockSpec((tm, D_MODEL), row)]
    args = [x]
    scratch = []
    if moe:
        in_specs += [pl.BlockSpec((TOP_K * tm,), lambda i: (i,), memory_space=pltpu.SMEM),
                     pl.BlockSpec(memory_space=pl.ANY),
                     pl.BlockSpec((tm, LANES), row)]
        args += [dst, ys, route]
        scratch = [pltpu.VMEM((TOP_K, tm, D_MODEL), F32), pltpu.SemaphoreType.DMA(())]
    in_specs += [
        pl.BlockSpec((None, tm, PLE_DIM), lambda i: (layer, i, 0)),
        pl.BlockSpec(wg.shape, full),
        pl.BlockSpec((1, D_MODEL), full),
        pl.BlockSpec(pw.shape, full),
        pl.BlockSpec((1, D_MODEL), full),
    ]
    args += [p, wg, b, pw, pn]
    if last:
        in_specs.append(pl.BlockSpec((1, D_MODEL), full))
        args.append(final_norm)
    return pl.pallas_call(
        functools.partial(_ple_kernel, moe=moe, last=last),
        grid=(T // tm,),
        in_specs=in_specs,
        out_specs=pl.BlockSpec((tm, D_MODEL), row),
        out_shape=jax.ShapeDtypeStruct((T, D_MODEL), F32),
        scratch_shapes=scratch,
        compiler_params=_params(("parallel",)),
        name="ple" + ("_moe" if moe else "") + ("_last" if last else ""),
    )(*args)


def _rope_tables(seq):
    half = ROT_DIM // 2
    pos = jnp.arange(seq, dtype=F32)
    inv = ROPE_THETA ** (-jnp.arange(half, dtype=F32) * 2.0 / ROT_DIM)
    ang = pos[:, None] * inv[None, :]
    cos, sin = jnp.cos(ang), jnp.sin(ang)
    ones = jnp.ones((seq, HEAD_DIM - ROT_DIM), F32)
    zeros = jnp.zeros((seq, HEAD_DIM - ROT_DIM), F32)
    zh = jnp.zeros((seq, half), F32)
    c = jnp.concatenate([cos, cos, ones], axis=1)
    sa = jnp.concatenate([zh, sin, zeros], axis=1)
    sb = jnp.concatenate([-sin, zh, zeros], axis=1)
    rep = lambda t: jnp.concatenate([t, t], axis=1)
    return rep(c), rep(sa), rep(sb)


def _group_layout(route, counts, tm):
    T = route.shape[0]
    counts = counts[0, :N_EXPERTS].astype(jnp.int32)
    n_tile_e = (counts + tm - 1) // tm
    tile_end = jnp.cumsum(n_tile_e)
    start = (tile_end - n_tile_e) * tm
    expert = route[:, 0:TOP_K].astype(jnp.int32)
    rank = route[:, 4:4 + TOP_K].astype(jnp.int32)
    onehot = expert[..., None] == jnp.arange(N_EXPERTS, dtype=jnp.int32)
    dst = (jnp.sum(jnp.where(onehot, start, 0), axis=-1) + rank).reshape(-1)
    n_tiles = TOP_K * T // tm + N_EXPERTS
    tile_expert = jnp.minimum(
        jnp.sum((jnp.arange(n_tiles, dtype=jnp.int32)[:, None] >= tile_end[None, :])
                .astype(jnp.int32), axis=1), N_EXPERTS - 1)
    return dst, tile_expert, tile_end[-1:], n_tiles * tm


def _trunk(x, p, W):
    B, S, _ = x.shape
    T = B * S
    depth = p.shape[0]
    x = x.reshape(T, D_MODEL)
    p = p.reshape(depth, T, PLE_DIM)
    cos, sa, sb = _rope_tables(S)
    for i in range(depth):
        lam_init = 0.8 - 0.6 * math.exp(-0.3 * i)
        q, k, v, gm = _inproj(x, W["norm_mix"][i], W["w_in"][i], cos, sa, sb, W["gmlp_ln_g"][i],
                              W["gmlp_ln_b"][i], W["gmlp_ws"][i], W["gmlp_bs"][i], S)
        att = _attention(q, k, v, W["lam"][i], W["subln"][i], B, S, lam_init)
        moe = i % 2 == 1
        j = i // 2
        fin = W["final_norm"] if i == depth - 1 else None
        ple_w = (W["ple_gate_w"][i], W["ple_gate_b"][i], W["ple_w"][i], W["ple_norm"][i], fin)
        if not moe:
            x1, h2 = _outproj(att, gm, x, W["w_out"][i], W["norm_ffn"][i], None)
            x2 = _dense_ffn(h2, x1, W["dense_w_gate"][j], W["dense_w_up"][j], W["dense_w_down"][j])
            x = _ple(x2, None, None, None, p, i, *ple_w)
        else:
            x1, hp, route, counts = _outproj(att, gm, x, W["w_out"][i], W["norm_ffn"][i],
                                             W["router"][j])
            tm = _tile(T, MOE_TM)
            dst, tile_expert, n_used, n_rows = _group_layout(route, counts, tm)
            xs = _moe_dispatch(hp, dst, n_rows)
            ys = _moe_ffn(xs, tile_expert, n_used, W["moe_w_gate"][j], W["moe_w_up"][j],
                          W["moe_w_down"][j], tm)
            x = _ple(x1, dst, ys, route, p, i, *ple_w)
    return x.reshape(B, S, D_MODEL)


def kernel(x_prompt, x_sample, p_prompt, p_sample, norm_mix, w_in, lambda_q1, lambda_k1, lambda_q2, lambda_k2, subln, gmlp_ln_g, gmlp_ln_b, gmlp_ws, gmlp_bs, w_out, norm_ffn, dense_w_gate, dense_w_up, dense_w_down, router, moe_w_gate, moe_w_up, moe_w_down, ple_w, ple_norm, ple_gate_w, ple_gate_b, final_norm):
    depth = w_in.shape[0]
    row = lambda t: t[:, None, :]
    lam = jnp.stack([lambda_q1, lambda_k1, lambda_q2, lambda_k2], axis=1)
    lam = jnp.pad(lam, ((0, 0), (0, 4), (0, LANES - HEAD_DIM)))
    r_hi = router.astype(BF16)
    r_lo = (router - r_hi.astype(F32)).astype(BF16)
    W = dict(
        norm_mix=row(norm_mix), w_in=w_in.astype(BF16), lam=lam, subln=row(subln),
        gmlp_ln_g=row(gmlp_ln_g), gmlp_ln_b=row(gmlp_ln_b), gmlp_ws=gmlp_ws.astype(BF16),
        gmlp_bs=jnp.broadcast_to(gmlp_bs[..., None], gmlp_bs.shape + (CHUNK,)),
        w_out=w_out.astype(BF16), norm_ffn=row(norm_ffn),
        dense_w_gate=dense_w_gate.astype(BF16), dense_w_up=dense_w_up.astype(BF16),
        dense_w_down=dense_w_down.astype(BF16),
        router=jnp.pad(jnp.concatenate([r_hi, r_lo], axis=-1),
                       ((0, 0), (0, 0), (0, LANES - 2 * N_EXPERTS))),
        moe_w_gate=moe_w_gate.astype(BF16), moe_w_up=moe_w_up.astype(BF16),
        moe_w_down=moe_w_down.astype(BF16),
        ple_w=ple_w.astype(BF16), ple_norm=row(ple_norm), ple_gate_w=ple_gate_w.astype(BF16),
        ple_gate_b=row(ple_gate_b), final_norm=final_norm[None, :],
    )
    return (_trunk(x_prompt, p_prompt, W), _trunk(x_sample, p_sample, W))
```

```python
import functools
import math

import jax
import jax.numpy as jnp
from jax import lax
from jax.experimental import pallas as pl
from jax.experimental.pallas import tpu as pltpu
from jax.experimental.pallas import tpu_sc as plsc

F32 = jnp.float32
BF16 = jnp.bfloat16

D_MODEL = 1024
ATTN_WIDTH = 512
GMLP_WIDTH = 512
HEAD_DIM = 64
HEAD_WIDTH = 2 * HEAD_DIM
N_HEADS = ATTN_WIDTH // HEAD_WIDTH
ROT_DIM = HEAD_DIM // 4
ROPE_THETA = 500000.0
CHUNK = 128
N_GROUPS = 4
D_FF = 3584
N_EXPERTS = 8
TOP_K = 2
PLE_DIM = 256
EPS = 1e-6
LANES = 128
Q_SCALE = (1.0 / math.sqrt(HEAD_DIM)) * math.log2(math.e)
VMEM_LIMIT = 48 * 1024 * 1024
ATTN_TQ = 1024
ATTN_TK = 512
ATTN_COLS = 256
FFN_TM = 512
MOE_TM = 512
FFN_ROW_SPLIT = 2
SC_WINDOW = 32
ROW_TM = 1024
ROW_SPLIT = 1


def _tile(n, pref):
    t = min(n, pref)
    while n % t:
        t //= 2
    return t


def _params(sem, **kw):
    return pltpu.CompilerParams(dimension_semantics=sem, vmem_limit_bytes=VMEM_LIMIT, **kw)


def _rms(x):
    return x * lax.rsqrt(jnp.mean(x * x, axis=-1, keepdims=True) + EPS)


def _inproj_kernel(x_ref, g_ref, w_ref, cos_ref, sa_ref, sb_ref, lng_ref, lnb_ref, ws_ref, bs_ref,
                   q_ref, k_ref, v_ref, gm_ref):
    tm = x_ref.shape[0]
    rb = tm // ROW_SPLIT
    half = ROT_DIM // 2
    for s in range(ROW_SPLIT):
        rows = slice(s * rb, (s + 1) * rb)
        h = (_rms(x_ref[rows, :]) * g_ref[...]).astype(BF16)

        def proj(c0, n):
            return jnp.dot(h, w_ref[:, c0:c0 + n], preferred_element_type=F32)

        cos, sa, sb = cos_ref[rows, :], sa_ref[rows, :], sb_ref[rows, :]

        def rope(t):
            outs = []
            for hh in range(N_HEADS):
                th = t[:, hh * HEAD_WIDTH:(hh + 1) * HEAD_WIDTH]
                outs.append(th * cos + pltpu.roll(th, half, 1) * sa
                            + pltpu.roll(th, HEAD_WIDTH - half, 1) * sb)
            return jnp.concatenate(outs, axis=1)

        q_ref[rows, :] = (rope(proj(0, ATTN_WIDTH)) * Q_SCALE).astype(BF16)
        k_ref[rows, :] = rope(proj(ATTN_WIDTH, ATTN_WIDTH)).astype(BF16)
        v_ref[rows, :] = proj(2 * ATTN_WIDTH, ATTN_WIDTH).astype(BF16)

        u = jax.nn.gelu(proj(3 * ATTN_WIDTH, GMLP_WIDTH))
        vf = jax.nn.gelu(proj(3 * ATTN_WIDTH + GMLP_WIDTH, GMLP_WIDTH))
        mu = jnp.mean(vf, axis=-1, keepdims=True)
        vc = vf - mu
        var = jnp.mean(vc * vc, axis=-1, keepdims=True)
        vn = (vc * lax.rsqrt(var + EPS) * lng_ref[...] + lnb_ref[...]).astype(BF16)
        for c in range(rb // CHUNK):
            r = slice(c * CHUNK, (c + 1) * CHUNK)
            ro = slice(s * rb + c * CHUNK, s * rb + (c + 1) * CHUNK)
            for g in range(N_GROUPS):
                cg = slice(g * CHUNK, (g + 1) * CHUNK)
                mixed = jnp.dot(ws_ref[g], vn[r, cg], preferred_element_type=F32) + bs_ref[g]
                gm_ref[ro, cg] = (u[r, cg] * mixed).astype(BF16)


def _inproj(x, g, w, cos, sa, sb, lng, lnb, ws, bsb, seq):
    T = x.shape[0]
    tm = _tile(seq, ROW_TM)
    ns = seq // tm
    full = lambda i: (0, 0)
    out = jax.ShapeDtypeStruct((T, ATTN_WIDTH), BF16)
    return pl.pallas_call(
        _inproj_kernel,
        grid=(T // tm,),
        in_specs=[
            pl.BlockSpec((tm, D_MODEL), lambda i: (i, 0)),
            pl.BlockSpec((1, D_MODEL), full),
            pl.BlockSpec(w.shape, full),
            pl.BlockSpec((tm, HEAD_WIDTH), lambda i: (i % ns, 0)),
            pl.BlockSpec((tm, HEAD_WIDTH), lambda i: (i % ns, 0)),
            pl.BlockSpec((tm, HEAD_WIDTH), lambda i: (i % ns, 0)),
            pl.BlockSpec((1, GMLP_WIDTH), full),
            pl.BlockSpec((1, GMLP_WIDTH), full),
            pl.BlockSpec(ws.shape, lambda i: (0, 0, 0)),
            pl.BlockSpec(bsb.shape, lambda i: (0, 0, 0)),
        ],
        out_specs=[pl.BlockSpec((tm, ATTN_WIDTH), lambda i: (i, 0))] * 4,
        out_shape=[out] * 4,
        compiler_params=_params(("parallel",)),
        name="inproj",
    )(x, g, w, cos, sa, sb, lng, lnb, ws, bsb)


def _attn_kernel(q_ref, k_ref, v_ref, lam_ref, sub_ref, o_ref,
                 qcat_ref, sa_ref, sb_ref, m_ref, l_ref, acc_ref, *, tk, lam_init):
    tq = q_ref.shape[0]
    n = k_ref.shape[0] // tk
    qt = q_ref[...].astype(F32).T
    row = lax.broadcasted_iota(jnp.int32, qt.shape, 0)
    qcat_ref[:, :tq] = jnp.where(row < HEAD_DIM, qt, 0.0).astype(BF16)
    qcat_ref[:, tq:] = jnp.where(row >= HEAD_DIM, qt, 0.0).astype(BF16)
    m_ref[...] = jnp.full(m_ref.shape, -jnp.inf, F32)
    l_ref[...] = jnp.zeros(l_ref.shape, F32)
    acc_ref[...] = jnp.zeros(acc_ref.shape, F32)

    def rows(t):
        return pl.ds(pl.multiple_of(t * tk, tk), tk)

    def scores(t, s_ref):
        s_ref[...] = jnp.dot(k_ref[rows(t), :], qcat_ref[...], preferred_element_type=F32)

    def update(t, s_ref):
        vt = v_ref[rows(t), :]
        for cb in range(2 * tq // ATTN_COLS):
            cols = slice(cb * ATTN_COLS, (cb + 1) * ATTN_COLS)
            s = s_ref[:, cols]
            m_old = m_ref[:, cols]
            m_new = jnp.maximum(m_old, jnp.max(s, axis=0, keepdims=True))
            alpha = jnp.exp2(m_old - m_new)
            p = jnp.exp2(s - m_new)
            l_ref[:, cols] = alpha * l_ref[:, cols] + jnp.sum(p, axis=0, keepdims=True)
            m_ref[:, cols] = m_new
            pv = lax.dot_general(vt, p.astype(BF16), (((0,), (0,)), ((), ())),
                                 preferred_element_type=F32)
            acc_ref[:, cols] = alpha * acc_ref[:, cols] + pv

    scores(0, sa_ref)

    def pair(u, carry):
        t = 2 * u
        scores(t + 1, sb_ref)
        update(t, sa_ref)
        scores(t + 2, sa_ref)
        update(t + 1, sb_ref)
        return carry

    lax.fori_loop(0, n // 2 - 1, pair, 0)
    scores(n - 1, sb_ref)
    update(n - 2, sa_ref)
    update(n - 1, sb_ref)

    lp = lam_ref[...]
    lam = (jnp.exp(jnp.sum(lp[0:1] * lp[1:2], axis=-1, keepdims=True))
           - jnp.exp(jnp.sum(lp[2:3] * lp[3:4], axis=-1, keepdims=True)) + lam_init)
    l = l_ref[...]
    ot = acc_ref[:, :tq] / l[:, :tq] - lam * (acc_ref[:, tq:] / l[:, tq:])
    o = ot.T
    o_ref[...] = (_rms(o) * sub_ref[...] * (1.0 - lam_init)).astype(o_ref.dtype)


def _attention(q, k, v, lam_p, sub, batch, seq, lam_init):
    T = q.shape[0]
    tq = _tile(seq, ATTN_TQ)
    tk = _tile(seq // 2, ATTN_TK)
    nq = seq // tq
    kv_spec = pl.BlockSpec((seq, HEAD_WIDTH), lambda b, h, i: (b, h))
    return pl.pallas_call(
        functools.partial(_attn_kernel, tk=tk, lam_init=lam_init),
        grid=(batch, N_HEADS, nq),
        in_specs=[
            pl.BlockSpec((tq, HEAD_WIDTH), lambda b, h, i: (b * nq + i, h)),
            kv_spec,
            kv_spec,
            pl.BlockSpec(lam_p.shape, lambda b, h, i: (0, 0)),
            pl.BlockSpec((1, HEAD_WIDTH), lambda b, h, i: (0, 0)),
        ],
        out_specs=pl.BlockSpec((tq, HEAD_WIDTH), lambda b, h, i: (b * nq + i, h)),
        out_shape=jax.ShapeDtypeStruct((T, ATTN_WIDTH), BF16),
        scratch_shapes=[
            pltpu.VMEM((HEAD_WIDTH, 2 * tq), BF16),
            pltpu.VMEM((tk, 2 * tq), F32),
            pltpu.VMEM((tk, 2 * tq), F32),
            pltpu.VMEM((1, 2 * tq), F32),
            pltpu.VMEM((1, 2 * tq), F32),
            pltpu.VMEM((HEAD_WIDTH, 2 * tq), F32),
        ],
        compiler_params=_params(("parallel", "parallel", "parallel")),
        name="attn",
    )(q, k, v, lam_p, sub)


def _pack_bf16_pairs(h):
    half = D_MODEL // 2
    h16 = h.astype(BF16)
    lo = lax.bitcast_convert_type(h16[:, :half].astype(F32), jnp.uint32)
    hi = lax.bitcast_convert_type(h16[:, half:].astype(F32), jnp.uint32)
    return (lo >> 16) | (hi & jnp.uint32(0xFFFF0000))


def _unpack_bf16_pairs(xp):
    lo = lax.bitcast_convert_type(xp << 16, F32)
    hi = lax.bitcast_convert_type(xp & jnp.uint32(0xFFFF0000), F32)
    return lo.astype(BF16), hi.astype(BF16)


def _outproj_kernel(att_ref, gm_ref, x_ref, w_ref, g_ref, *rest, moe):
    if moe:
        r_ref, x1_ref, h_ref, route_ref, count_ref, cnt_ref = rest
    else:
        x1_ref, h_ref = rest
    y = (jnp.dot(att_ref[...], w_ref[:ATTN_WIDTH, :], preferred_element_type=F32)
         + jnp.dot(gm_ref[...], w_ref[ATTN_WIDTH:, :], preferred_element_type=F32))
    x1 = x_ref[...] + y
    x1_ref[...] = x1
    h = _rms(x1) * g_ref[...]
    if not moe:
        h_ref[...] = h.astype(BF16)
        return
    h_ref[...] = _pack_bf16_pairs(h)
    tm = h.shape[0]
    h_hi = h.astype(BF16)
    h_lo = (h - h_hi.astype(F32)).astype(BF16)
    part = (jnp.dot(h_hi, r_ref[...], preferred_element_type=F32)
            + jnp.dot(h_lo, r_ref[...], preferred_element_type=F32))
    logits = part + pltpu.roll(part, LANES - N_EXPERTS, 1)
    lane = lax.broadcasted_iota(jnp.int32, logits.shape, 1)
    lg = jnp.where(lane < N_EXPERTS, logits, -jnp.inf)
    v1 = jnp.max(lg, axis=-1, keepdims=True)
    i1 = jnp.min(jnp.where(lg == v1, lane, LANES), axis=-1, keepdims=True)
    lg2 = jnp.where(lane == i1, -jnp.inf, lg)
    v2 = jnp.max(lg2, axis=-1, keepdims=True)
    i2 = jnp.min(jnp.where(lg2 == v2, lane, LANES), axis=-1, keepdims=True)
    e2 = jnp.exp(v2 - v1)
    den = 1.0 + e2

    @pl.when(pl.program_id(0) == 0)
    def _():
        cnt_ref[...] = jnp.zeros(cnt_ref.shape, F32)

    sel1 = lane == i1
    sel2 = lane == i2
    onehot = jnp.where(jnp.logical_or(sel1, sel2), 1.0, 0.0)
    r_i = lax.broadcasted_iota(jnp.int32, (tm, tm), 0)
    c_i = lax.broadcasted_iota(jnp.int32, (tm, tm), 1)
    tri = jnp.where(c_i < r_i, 1.0, 0.0).astype(BF16)
    before = jnp.dot(tri, onehot.astype(BF16), preferred_element_type=F32) + cnt_ref[...]
    rank1 = jnp.sum(jnp.where(sel1, before, 0.0), axis=-1, keepdims=True)
    rank2 = jnp.sum(jnp.where(sel2, before, 0.0), axis=-1, keepdims=True)
    cnt = cnt_ref[...] + jnp.sum(onehot, axis=0, keepdims=True)
    cnt_ref[...] = cnt
    count_ref[...] = jnp.broadcast_to(cnt, count_ref.shape)
    route = jnp.zeros(logits.shape, F32)
    for col, val in enumerate((i1.astype(F32), i2.astype(F32), 1.0 / den, e2 / den, rank1, rank2)):
        route = jnp.where(lane == col, val, route)
    route_ref[...] = route


def _outproj(att, gm, x, w, g, router_pad):
    T = x.shape[0]
    tm = _tile(T, 512)
    moe = router_pad is not None
    full = lambda i: (0, 0)
    row = lambda i: (i, 0)
    in_specs = [
        pl.BlockSpec((tm, ATTN_WIDTH), row),
        pl.BlockSpec((tm, GMLP_WIDTH), row),
        pl.BlockSpec((tm, D_MODEL), row),
        pl.BlockSpec(w.shape, full),
        pl.BlockSpec((1, D_MODEL), full),
    ]
    args = [att, gm, x, w, g]
    if not moe:
        out_specs = [pl.BlockSpec((tm, D_MODEL), row), pl.BlockSpec((tm, D_MODEL), row)]
        out_shape = [jax.ShapeDtypeStruct((T, D_MODEL), F32),
                     jax.ShapeDtypeStruct((T, D_MODEL), BF16)]
        scratch = []
    else:
        in_specs.append(pl.BlockSpec(router_pad.shape, full))
        args.append(router_pad)
        out_specs = [pl.BlockSpec((tm, D_MODEL), row), pl.BlockSpec((tm, D_MODEL // 2), row),
                     pl.BlockSpec((tm, LANES), row), pl.BlockSpec((8, LANES), full)]
        out_shape = [jax.ShapeDtypeStruct((T, D_MODEL), F32),
                     jax.ShapeDtypeStruct((T, D_MODEL // 2), jnp.uint32),
                     jax.ShapeDtypeStruct((T, LANES), F32),
                     jax.ShapeDtypeStruct((8, LANES), F32)]
        scratch = [pltpu.VMEM((1, LANES), F32)]
    return pl.pallas_call(
        functools.partial(_outproj_kernel, moe=moe),
        grid=(T // tm,),
        in_specs=in_specs,
        out_specs=out_specs,
        out_shape=out_shape,
        scratch_shapes=scratch,
        compiler_params=_params(("arbitrary",) if moe else ("parallel",)),
        name="outproj_moe" if moe else "outproj",
    )(*args)


def _dispatch_kernel(dst_ref, h_ref, xs_in_ref, xs_ref, sem):
    del xs_in_ref
    tg = h_ref.shape[0]

    def row_copy(r, k):
        return pltpu.make_async_copy(h_ref.at[pl.ds(r, 1), :],
                                     xs_ref.at[pl.ds(dst_ref[TOP_K * r + k], 1), :], sem)

    def issue(r, carry):
        for k in range(TOP_K):
            row_copy(r, k).start(priority=k)
        return carry

    lax.fori_loop(0, tg, issue, 0, unroll=8)
    for k in range(TOP_K):
        pltpu.make_async_copy(h_ref, xs_ref.at[pl.ds(0, tg), :], sem).wait()


def _moe_dispatch(hp, dst, n_rows):
    T = hp.shape[0]
    tg = _tile(T, 512)
    xs0 = jnp.zeros((n_rows, D_MODEL // 2), jnp.uint32)
    return pl.pallas_call(
        _dispatch_kernel,
        grid=(T // tg,),
        in_specs=[
            pl.BlockSpec((TOP_K * tg,), lambda i: (i,), memory_space=pltpu.SMEM),
            pl.BlockSpec((tg, D_MODEL // 2), lambda i: (i, 0)),
            pl.BlockSpec(memory_space=pl.ANY),
        ],
        out_specs=pl.BlockSpec(memory_space=pl.ANY),
        out_shape=jax.ShapeDtypeStruct(xs0.shape, xs0.dtype),
        scratch_shapes=[pltpu.SemaphoreType.DMA(())],
        input_output_aliases={2: 0},
        compiler_params=_params(("arbitrary",)),
        name="moe_dispatch",
    )(dst, hp, xs0)


def _swiglu_step(h_ref, wg_ref, wu_ref, wd_ref, acc_ref):
    tm = h_ref.shape[0]
    rb = tm // FFN_ROW_SPLIT

    @pl.when(pl.program_id(1) == 0)
    def _():
        acc_ref[...] = jnp.zeros(acc_ref.shape, F32)

    for s in range(FFN_ROW_SPLIT):
        rows = slice(s * rb, (s + 1) * rb)
        h = h_ref[rows, :]
        a = (jax.nn.silu(jnp.dot(h, wg_ref[...], preferred_element_type=F32))
             * jnp.dot(h, wu_ref[...], preferred_element_type=F32)).astype(BF16)
        acc_ref[rows, :] += jnp.dot(a, wd_ref[...], preferred_element_type=F32)


def _dense_ffn_kernel(h_ref, x_ref, wg_ref, wu_ref, wd_ref, o_ref, acc_ref):
    _swiglu_step(h_ref, wg_ref, wu_ref, wd_ref, acc_ref)

    @pl.when(pl.program_id(1) == pl.num_programs(1) - 1)
    def _():
        o_ref[...] = x_ref[...] + acc_ref[...]


def _dense_ffn(h, x, wg, wu, wd):
    T = h.shape[0]
    tm = _tile(T, FFN_TM)
    tf = D_FF // 2 if D_FF % 512 == 0 else _tile(D_FF, 512)
    row = lambda i, j: (i, 0)
    return pl.pallas_call(
        _dense_ffn_kernel,
        grid=(T // tm, D_FF // tf),
        in_specs=[
            pl.BlockSpec((tm, D_MODEL), row),
            pl.BlockSpec((tm, D_MODEL), row),
            pl.BlockSpec((D_MODEL, tf), lambda i, j: (0, j)),
            pl.BlockSpec((D_MODEL, tf), lambda i, j: (0, j)),
            pl.BlockSpec((tf, D_MODEL), lambda i, j: (j, 0)),
        ],
        out_specs=pl.BlockSpec((tm, D_MODEL), row),
        out_shape=jax.ShapeDtypeStruct((T, D_MODEL), F32),
        scratch_shapes=[pltpu.VMEM((tm, D_MODEL), F32)],
        compiler_params=_params(("parallel", "arbitrary")),
        name="dense_ffn",
    )(h, x, wg, wu, wd)


def _moe_ffn_kernel(te_ref, nu_ref, xp_ref, wg_ref, wu_ref, wd_ref, o_ref, acc_ref, h_ref):
    i = pl.program_id(0)
    last = pl.program_id(1) == pl.num_programs(1) - 1

    @pl.when(i < nu_ref[0])
    def _():
        @pl.when(pl.program_id(1) == 0)
        def _():
            lo, hi = _unpack_bf16_pairs(xp_ref[...])
            h_ref[:, :D_MODEL // 2] = lo
            h_ref[:, D_MODEL // 2:] = hi

        _swiglu_step(h_ref, wg_ref, wu_ref, wd_ref, acc_ref)

        @pl.when(last)
        def _():
            o_ref[...] = acc_ref[...]

    @pl.when(jnp.logical_and(i >= nu_ref[0], last))
    def _():
        o_ref[...] = jnp.zeros(o_ref.shape, o_ref.dtype)


def _moe_ffn(xs, tile_expert, n_used, wg, wu, wd, tm):
    P = xs.shape[0]
    tf = D_FF // 2 if D_FF % 512 == 0 else _tile(D_FF, 512)
    row = lambda i, j, te, nu: (i, 0)
    return pl.pallas_call(
        _moe_ffn_kernel,
        grid_spec=pltpu.PrefetchScalarGridSpec(
            num_scalar_prefetch=2,
            grid=(P // tm, D_FF // tf),
            in_specs=[
                pl.BlockSpec((tm, D_MODEL // 2), row),
                pl.BlockSpec((None, D_MODEL, tf), lambda i, j, te, nu: (te[i], 0, j)),
                pl.BlockSpec((None, D_MODEL, tf), lambda i, j, te, nu: (te[i], 0, j)),
                pl.BlockSpec((None, tf, D_MODEL), lambda i, j, te, nu: (te[i], j, 0)),
            ],
            out_specs=pl.BlockSpec((tm, D_MODEL), row),
            scratch_shapes=[pltpu.VMEM((tm, D_MODEL), F32), pltpu.VMEM((tm, D_MODEL), BF16)],
        ),
        out_shape=jax.ShapeDtypeStruct((P, D_MODEL), F32),
        compiler_params=_params(("arbitrary", "arbitrary")),
        name="moe_ffn",
    )(tile_expert, n_used, xs, wg, wu, wd)


def _sc_gather_rows(src, idx):
    n = idx.shape[0]
    info = plsc.get_sparse_core_info()
    n_workers = info.num_cores * info.num_subcores
    per_worker = n // n_workers
    mesh = plsc.VectorSubcoreMesh(core_axis_name="c", subcore_axis_name="s")

    @functools.partial(
        pl.kernel, mesh=mesh,
        out_type=jax.ShapeDtypeStruct((n, D_MODEL), src.dtype),
        scratch_types=[pltpu.VMEM((SC_WINDOW,), jnp.int32),
                       pltpu.VMEM((SC_WINDOW, D_MODEL), src.dtype),
                       pltpu.SemaphoreType.DMA])
    def gather(src_hbm, idx_hbm, out_hbm, idx_v, rows_v, sem):
        wid = lax.axis_index("s") * info.num_cores + lax.axis_index("c")
        base = wid * per_worker

        @pl.loop(0, per_worker, step=SC_WINDOW)
        def _(off):
            pltpu.sync_copy(idx_hbm.at[pl.ds(base + off, SC_WINDOW)], idx_v)
            pltpu.async_copy(src_hbm.at[idx_v], rows_v, sem).wait()
            pltpu.sync_copy(rows_v, out_hbm.at[pl.ds(base + off, SC_WINDOW)])

    return gather(src, idx)


def _ple_kernel(*refs, moe, last):
    refs = list(refs)
    x_ref = refs.pop(0)
    if moe:
        y0_ref, y1_ref, route_ref = refs.pop(0), refs.pop(0), refs.pop(0)
    p_ref, wg_ref, b_ref, pw_ref, pn_ref = refs[:5]
    refs = refs[5:]
    if last:
        fn_ref = refs.pop(0)
    o_ref = refs.pop(0)

    e = jnp.dot(p_ref[...].astype(BF16), pw_ref[...], preferred_element_type=F32)
    e = _rms(e) * pn_ref[...]
    x = x_ref[...]
    if moe:
        rt = route_ref[...]
        x = x + (rt[:, 2:3] * y0_ref[...] + rt[:, 3:4] * y1_ref[...])
    gate = jax.nn.sigmoid(jnp.dot(_rms(x).astype(BF16), wg_ref[...], preferred_element_type=F32)
                          + b_ref[...])
    x = x + gate * e
    if last:
        x = _rms(x) * fn_ref[...]
    o_ref[...] = x


def _ple(x, dst, ys, route, p, layer, wg, b, pw, pn, final_norm):
    T = x.shape[0]
    tm = _tile(T, 512)
    moe = ys is not None
    last = final_norm is not None
    full = lambda i: (0, 0)
    row = lambda i: (i, 0)
    in_specs = [pl.BlockSpec((tm, D_MODEL), row)]
    args = [x]
    scratch = []
    if moe:
        dst2 = dst.reshape(T, TOP_K)
        in_specs += [pl.BlockSpec((tm, D_MODEL), row), pl.BlockSpec((tm, D_MODEL), row),
                     pl.BlockSpec((tm, LANES), row)]
        args += [_sc_gather_rows(ys, dst2[:, 0]), _sc_gather_rows(ys, dst2[:, 1]), route]
    in_specs += [
        pl.BlockSpec((None, tm, PLE_DIM), lambda i: (layer, i, 0)),
        pl.BlockSpec(wg.shape, full),
        pl.BlockSpec((1, D_MODEL), full),
        pl.BlockSpec(pw.shape, full),
        pl.BlockSpec((1, D_MODEL), full),
    ]
    args += [p, wg, b, pw, pn]
    if last:
        in_specs.append(pl.BlockSpec((1, D_MODEL), full))
        args.append(final_norm)
    return pl.pallas_call(
        functools.partial(_ple_kernel, moe=moe, last=last),
        grid=(T // tm,),
        in_specs=in_specs,
        out_specs=pl.BlockSpec((tm, D_MODEL), row),
        out_shape=jax.ShapeDtypeStruct((T, D_MODEL), F32),
        scratch_shapes=scratch,
        compiler_params=_params(("parallel",)),
        name="ple" + ("_moe" if moe else "") + ("_last" if last else ""),
    )(*args)


def _rope_tables(seq):
    half = ROT_DIM // 2
    pos = jnp.arange(seq, dtype=F32)
    inv = ROPE_THETA ** (-jnp.arange(half, dtype=F32) * 2.0 / ROT_DIM)
    ang = pos[:, None] * inv[None, :]
    cos, sin = jnp.cos(ang), jnp.sin(ang)
    ones = jnp.ones((seq, HEAD_DIM - ROT_DIM), F32)
    zeros = jnp.zeros((seq, HEAD_DIM - ROT_DIM), F32)
    zh = jnp.zeros((seq, half), F32)
    c = jnp.concatenate([cos, cos, ones], axis=1)
    sa = jnp.concatenate([zh, sin, zeros], axis=1)
    sb = jnp.concatenate([-sin, zh, zeros], axis=1)
    rep = lambda t: jnp.concatenate([t, t], axis=1)
    return rep(c), rep(sa), rep(sb)


def _group_layout(route, counts, tm):
    T = route.shape[0]
    counts = counts[0, :N_EXPERTS].astype(jnp.int32)
    n_tile_e = (counts + tm - 1) // tm
    tile_end = jnp.cumsum(n_tile_e)
    start = (tile_end - n_tile_e) * tm
    expert = route[:, 0:TOP_K].astype(jnp.int32)
    rank = route[:, 4:4 + TOP_K].astype(jnp.int32)
    onehot = expert[..., None] == jnp.arange(N_EXPERTS, dtype=jnp.int32)
    dst = (jnp.sum(jnp.where(onehot, start, 0), axis=-1) + rank).reshape(-1)
    n_tiles = TOP_K * T // tm + N_EXPERTS
    tile_expert = jnp.minimum(
        jnp.sum((jnp.arange(n_tiles, dtype=jnp.int32)[:, None] >= tile_end[None, :])
                .astype(jnp.int32), axis=1), N_EXPERTS - 1)
    return dst, tile_expert, tile_end[-1:], n_tiles * tm


def _trunk(x, p, W):
    B, S, _ = x.shape
    T = B * S
    depth = p.shape[0]
    x = x.reshape(T, D_MODEL)
    p = p.reshape(depth, T, PLE_DIM)
    cos, sa, sb = _rope_tables(S)
    for i in range(depth):
        lam_init = 0.8 - 0.6 * math.exp(-0.3 * i)
        q, k, v, gm = _inproj(x, W["norm_mix"][i], W["w_in"][i], cos, sa, sb, W["gmlp_ln_g"][i],
                              W["gmlp_ln_b"][i], W["gmlp_ws"][i], W["gmlp_bs"][i], S)
        att = _attention(q, k, v, W["lam"][i], W["subln"][i], B, S, lam_init)
        moe = i % 2 == 1
        j = i // 2
        fin = W["final_norm"] if i == depth - 1 else None
        ple_w = (W["ple_gate_w"][i], W["ple_gate_b"][i], W["ple_w"][i], W["ple_norm"][i], fin)
        if not moe:
            x1, h2 = _outproj(att, gm, x, W["w_out"][i], W["norm_ffn"][i], None)
            x2 = _dense_ffn(h2, x1, W["dense_w_gate"][j], W["dense_w_up"][j], W["dense_w_down"][j])
            x = _ple(x2, None, None, None, p, i, *ple_w)
        else:
            x1, hp, route, counts = _outproj(att, gm, x, W["w_out"][i], W["norm_ffn"][i],
                                             W["router"][j])
            tm = _tile(T, MOE_TM)
            dst, tile_expert, n_used, n_rows = _group_layout(route, counts, tm)
            xs = _moe_dispatch(hp, dst, n_rows)
            ys = _moe_ffn(xs, tile_expert, n_used, W["moe_w_gate"][j], W["moe_w_up"][j],
                          W["moe_w_down"][j], tm)
            x = _ple(x1, dst, ys, route, p, i, *ple_w)
    return x.reshape(B, S, D_MODEL)


def kernel(x_prompt, x_sample, p_prompt, p_sample, norm_mix, w_in, lambda_q1, lambda_k1, lambda_q2, lambda_k2, subln, gmlp_ln_g, gmlp_ln_b, gmlp_ws, gmlp_bs, w_out, norm_ffn, dense_w_gate, dense_w_up, dense_w_down, router, moe_w_gate, moe_w_up, moe_w_down, ple_w, ple_norm, ple_gate_w, ple_gate_b, final_norm):
    depth = w_in.shape[0]
    row = lambda t: t[:, None, :]
    lam = jnp.stack([lambda_q1, lambda_k1, lambda_q2, lambda_k2], axis=1)
    lam = jnp.pad(lam, ((0, 0), (0, 4), (0, LANES - HEAD_DIM)))
    r_hi = router.astype(BF16)
    r_lo = (router - r_hi.astype(F32)).astype(BF16)
    W = dict(
        norm_mix=row(norm_mix), w_in=w_in.astype(BF16), lam=lam, subln=row(subln),
        gmlp_ln_g=row(gmlp_ln_g), gmlp_ln_b=row(gmlp_ln_b), gmlp_ws=gmlp_ws.astype(BF16),
        gmlp_bs=jnp.broadcast_to(gmlp_bs[..., None], gmlp_bs.shape + (CHUNK,)),
        w_out=w_out.astype(BF16), norm_ffn=row(norm_ffn),
        dense_w_gate=dense_w_gate.astype(BF16), dense_w_up=dense_w_up.astype(BF16),
        dense_w_down=dense_w_down.astype(BF16),
        router=jnp.pad(jnp.concatenate([r_hi, r_lo], axis=-1),
                       ((0, 0), (0, 0), (0, LANES - 2 * N_EXPERTS))),
        moe_w_gate=moe_w_gate.astype(BF16), moe_w_up=moe_w_up.astype(BF16),
        moe_w_down=moe_w_down.astype(BF16),
        ple_w=ple_w.astype(BF16), ple_norm=row(ple_norm), ple_gate_w=ple_gate_w.astype(BF16),
        ple_gate_b=row(ple_gate_b), final_norm=final_norm[None, :],
    )
    return (_trunk(x_prompt, p_prompt, W), _trunk(x_sample, p_sample, W))
```

```python
import functools
import math

import jax
import jax.numpy as jnp
from jax import lax
from jax.experimental import pallas as pl
from jax.experimental.pallas import tpu as pltpu
from jax.experimental.pallas import tpu_sc as plsc

F32 = jnp.float32
BF16 = jnp.bfloat16

D_MODEL = 1024
ATTN_WIDTH = 512
GMLP_WIDTH = 512
HEAD_DIM = 64
HEAD_WIDTH = 2 * HEAD_DIM
N_HEADS = ATTN_WIDTH // HEAD_WIDTH
ROT_DIM = HEAD_DIM // 4
ROPE_THETA = 500000.0
CHUNK = 128
N_GROUPS = 4
D_FF = 3584
N_EXPERTS = 8
TOP_K = 2
PLE_DIM = 256
EPS = 1e-6
LANES = 128
Q_SCALE = (1.0 / math.sqrt(HEAD_DIM)) * math.log2(math.e)
VMEM_LIMIT = 48 * 1024 * 1024
ATTN_TQ = 1024
ATTN_TK = 512
ATTN_COLS = 256
FFN_TM = 512
MOE_TM = 512
FFN_ROW_SPLIT = 2
SC_WINDOW = 32
SC_SCATTER_WINDOW = 128
ROW_TM = 1024
ROW_SPLIT = 1


def _tile(n, pref):
    t = min(n, pref)
    while n % t:
        t //= 2
    return t


def _params(sem, **kw):
    return pltpu.CompilerParams(dimension_semantics=sem, vmem_limit_bytes=VMEM_LIMIT, **kw)


def _rms(x):
    return x * lax.rsqrt(jnp.mean(x * x, axis=-1, keepdims=True) + EPS)


def _inproj_kernel(x_ref, g_ref, w_ref, cos_ref, sa_ref, sb_ref, lng_ref, lnb_ref, ws_ref, bs_ref,
                   q_ref, k_ref, v_ref, gm_ref):
    tm = x_ref.shape[0]
    rb = tm // ROW_SPLIT
    half = ROT_DIM // 2
    for s in range(ROW_SPLIT):
        rows = slice(s * rb, (s + 1) * rb)
        h = (_rms(x_ref[rows, :]) * g_ref[...]).astype(BF16)

        def proj(c0, n):
            return jnp.dot(h, w_ref[:, c0:c0 + n], preferred_element_type=F32)

        cos, sa, sb = cos_ref[rows, :], sa_ref[rows, :], sb_ref[rows, :]

        def rope(t):
            outs = []
            for hh in range(N_HEADS):
                th = t[:, hh * HEAD_WIDTH:(hh + 1) * HEAD_WIDTH]
                outs.append(th * cos + pltpu.roll(th, half, 1) * sa
                            + pltpu.roll(th, HEAD_WIDTH - half, 1) * sb)
            return jnp.concatenate(outs, axis=1)

        q_ref[rows, :] = (rope(proj(0, ATTN_WIDTH)) * Q_SCALE).astype(BF16)
        k_ref[rows, :] = rope(proj(ATTN_WIDTH, ATTN_WIDTH)).astype(BF16)
        v_ref[rows, :] = proj(2 * ATTN_WIDTH, ATTN_WIDTH).astype(BF16)

        u = jax.nn.gelu(proj(3 * ATTN_WIDTH, GMLP_WIDTH))
        vf = jax.nn.gelu(proj(3 * ATTN_WIDTH + GMLP_WIDTH, GMLP_WIDTH))
        mu = jnp.mean(vf, axis=-1, keepdims=True)
        vc = vf - mu
        var = jnp.mean(vc * vc, axis=-1, keepdims=True)
        vn = (vc * lax.rsqrt(var + EPS) * lng_ref[...] + lnb_ref[...]).astype(BF16)
        for c in range(rb // CHUNK):
            r = slice(c * CHUNK, (c + 1) * CHUNK)
            ro = slice(s * rb + c * CHUNK, s * rb + (c + 1) * CHUNK)
            for g in range(N_GROUPS):
                cg = slice(g * CHUNK, (g + 1) * CHUNK)
                mixed = jnp.dot(ws_ref[g], vn[r, cg], preferred_element_type=F32) + bs_ref[g]
                gm_ref[ro, cg] = (u[r, cg] * mixed).astype(BF16)


def _inproj(x, g, w, cos, sa, sb, lng, lnb, ws, bsb, seq):
    T = x.shape[0]
    tm = _tile(seq, ROW_TM)
    ns = seq // tm
    full = lambda i: (0, 0)
    out = jax.ShapeDtypeStruct((T, ATTN_WIDTH), BF16)
    return pl.pallas_call(
        _inproj_kernel,
        grid=(T // tm,),
        in_specs=[
            pl.BlockSpec((tm, D_MODEL), lambda i: (i, 0)),
            pl.BlockSpec((1, D_MODEL), full),
            pl.BlockSpec(w.shape, full),
            pl.BlockSpec((tm, HEAD_WIDTH), lambda i: (i % ns, 0)),
            pl.BlockSpec((tm, HEAD_WIDTH), lambda i: (i % ns, 0)),
            pl.BlockSpec((tm, HEAD_WIDTH), lambda i: (i % ns, 0)),
            pl.BlockSpec((1, GMLP_WIDTH), full),
            pl.BlockSpec((1, GMLP_WIDTH), full),
            pl.BlockSpec(ws.shape, lambda i: (0, 0, 0)),
            pl.BlockSpec(bsb.shape, lambda i: (0, 0, 0)),
        ],
        out_specs=[pl.BlockSpec((tm, ATTN_WIDTH), lambda i: (i, 0))] * 4,
        out_shape=[out] * 4,
        compiler_params=_params(("parallel",)),
        name="inproj",
    )(x, g, w, cos, sa, sb, lng, lnb, ws, bsb)


def _attn_kernel(q_ref, k_ref, v_ref, lam_ref, sub_ref, o_ref,
                 qcat_ref, sa_ref, sb_ref, m_ref, l_ref, acc_ref, *, tk, lam_init):
    tq = q_ref.shape[0]
    n = k_ref.shape[0] // tk
    qt = q_ref[...].astype(F32).T
    row = lax.broadcasted_iota(jnp.int32, qt.shape, 0)
    qcat_ref[:, :tq] = jnp.where(row < HEAD_DIM, qt, 0.0).astype(BF16)
    qcat_ref[:, tq:] = jnp.where(row >= HEAD_DIM, qt, 0.0).astype(BF16)
    m_ref[...] = jnp.full(m_ref.shape, -jnp.inf, F32)
    l_ref[...] = jnp.zeros(l_ref.shape, F32)
    acc_ref[...] = jnp.zeros(acc_ref.shape, F32)

    def rows(t):
        return pl.ds(pl.multiple_of(t * tk, tk), tk)

    def scores(t, s_ref):
        s_ref[...] = jnp.dot(k_ref[rows(t), :], qcat_ref[...], preferred_element_type=F32)

    def update(t, s_ref):
        vt = v_ref[rows(t), :]
        for cb in range(2 * tq // ATTN_COLS):
            cols = slice(cb * ATTN_COLS, (cb + 1) * ATTN_COLS)
            s = s_ref[:, cols]
            m_old = m_ref[:, cols]
            m_new = jnp.maximum(m_old, jnp.max(s, axis=0, keepdims=True))
            alpha = jnp.exp2(m_old - m_new)
            p = jnp.exp2(s - m_new)
            l_ref[:, cols] = alpha * l_ref[:, cols] + jnp.sum(p, axis=0, keepdims=True)
            m_ref[:, cols] = m_new
            pv = lax.dot_general(vt, p.astype(BF16), (((0,), (0,)), ((), ())),
                                 preferred_element_type=F32)
            acc_ref[:, cols] = alpha * acc_ref[:, cols] + pv

    scores(0, sa_ref)

    def pair(u, carry):
        t = 2 * u
        scores(t + 1, sb_ref)
        update(t, sa_ref)
        scores(t + 2, sa_ref)
        update(t + 1, sb_ref)
        return carry

    lax.fori_loop(0, n // 2 - 1, pair, 0)
    scores(n - 1, sb_ref)
    update(n - 2, sa_ref)
    update(n - 1, sb_ref)

    lp = lam_ref[...]
    lam = (jnp.exp(jnp.sum(lp[0:1] * lp[1:2], axis=-1, keepdims=True))
           - jnp.exp(jnp.sum(lp[2:3] * lp[3:4], axis=-1, keepdims=True)) + lam_init)
    l = l_ref[...]
    ot = acc_ref[:, :tq] / l[:, :tq] - lam * (acc_ref[:, tq:] / l[:, tq:])
    o = ot.T
    o_ref[...] = (_rms(o) * sub_ref[...] * (1.0 - lam_init)).astype(o_ref.dtype)


def _attention(q, k, v, lam_p, sub, batch, seq, lam_init):
    T = q.shape[0]
    tq = _tile(seq, ATTN_TQ)
    tk = _tile(seq // 2, ATTN_TK)
    nq = seq // tq
    kv_spec = pl.BlockSpec((seq, HEAD_WIDTH), lambda b, h, i: (b, h))
    return pl.pallas_call(
        functools.partial(_attn_kernel, tk=tk, lam_init=lam_init),
        grid=(batch, N_HEADS, nq),
        in_specs=[
            pl.BlockSpec((tq, HEAD_WIDTH), lambda b, h, i: (b * nq + i, h)),
            kv_spec,
            kv_spec,
            pl.BlockSpec(lam_p.shape, lambda b, h, i: (0, 0)),
            pl.BlockSpec((1, HEAD_WIDTH), lambda b, h, i: (0, 0)),
        ],
        out_specs=pl.BlockSpec((tq, HEAD_WIDTH), lambda b, h, i: (b * nq + i, h)),
        out_shape=jax.ShapeDtypeStruct((T, ATTN_WIDTH), BF16),
        scratch_shapes=[
            pltpu.VMEM((HEAD_WIDTH, 2 * tq), BF16),
            pltpu.VMEM((tk, 2 * tq), F32),
            pltpu.VMEM((tk, 2 * tq), F32),
            pltpu.VMEM((1, 2 * tq), F32),
            pltpu.VMEM((1, 2 * tq), F32),
            pltpu.VMEM((HEAD_WIDTH, 2 * tq), F32),
        ],
        compiler_params=_params(("parallel", "parallel", "parallel")),
        name="attn",
    )(q, k, v, lam_p, sub)


def _pack_bf16_pairs(h):
    half = D_MODEL // 2
    h16 = h.astype(BF16)
    lo = lax.bitcast_convert_type(h16[:, :half].astype(F32), jnp.uint32)
    hi = lax.bitcast_convert_type(h16[:, half:].astype(F32), jnp.uint32)
    return (lo >> 16) | (hi & jnp.uint32(0xFFFF0000))


def _unpack_bf16_pairs(xp):
    lo = lax.bitcast_convert_type(xp << 16, F32)
    hi = lax.bitcast_convert_type(xp & jnp.uint32(0xFFFF0000), F32)
    return lo.astype(BF16), hi.astype(BF16)


def _outproj_kernel(att_ref, gm_ref, x_ref, w_ref, g_ref, *rest, moe):
    if moe:
        r_ref, x1_ref, h_ref, route_ref, count_ref, cnt_ref = rest
    else:
        x1_ref, h_ref = rest
    y = (jnp.dot(att_ref[...], w_ref[:ATTN_WIDTH, :], preferred_element_type=F32)
         + jnp.dot(gm_ref[...], w_ref[ATTN_WIDTH:, :], preferred_element_type=F32))
    x1 = x_ref[...] + y
    x1_ref[...] = x1
    h = _rms(x1) * g_ref[...]
    if not moe:
        h_ref[...] = h.astype(BF16)
        return
    h_ref[...] = _pack_bf16_pairs(h)
    tm = h.shape[0]
    h_hi = h.astype(BF16)
    h_lo = (h - h_hi.astype(F32)).astype(BF16)
    part = (jnp.dot(h_hi, r_ref[...], preferred_element_type=F32)
            + jnp.dot(h_lo, r_ref[...], preferred_element_type=F32))
    logits = part + pltpu.roll(part, LANES - N_EXPERTS, 1)
    lane = lax.broadcasted_iota(jnp.int32, logits.shape, 1)
    lg = jnp.where(lane < N_EXPERTS, logits, -jnp.inf)
    v1 = jnp.max(lg, axis=-1, keepdims=True)
    i1 = jnp.min(jnp.where(lg == v1, lane, LANES), axis=-1, keepdims=True)
    lg2 = jnp.where(lane == i1, -jnp.inf, lg)
    v2 = jnp.max(lg2, axis=-1, keepdims=True)
    i2 = jnp.min(jnp.where(lg2 == v2, lane, LANES), axis=-1, keepdims=True)
    e2 = jnp.exp(v2 - v1)
    den = 1.0 + e2

    @pl.when(pl.program_id(0) == 0)
    def _():
        cnt_ref[...] = jnp.zeros(cnt_ref.shape, F32)

    sel1 = lane == i1
    sel2 = lane == i2
    onehot = jnp.where(jnp.logical_or(sel1, sel2), 1.0, 0.0)
    r_i = lax.broadcasted_iota(jnp.int32, (tm, tm), 0)
    c_i = lax.broadcasted_iota(jnp.int32, (tm, tm), 1)
    tri = jnp.where(c_i < r_i, 1.0, 0.0).astype(BF16)
    before = jnp.dot(tri, onehot.astype(BF16), preferred_element_type=F32) + cnt_ref[...]
    rank1 = jnp.sum(jnp.where(sel1, before, 0.0), axis=-1, keepdims=True)
    rank2 = jnp.sum(jnp.where(sel2, before, 0.0), axis=-1, keepdims=True)
    cnt = cnt_ref[...] + jnp.sum(onehot, axis=0, keepdims=True)
    cnt_ref[...] = cnt
    count_ref[...] = jnp.broadcast_to(cnt, count_ref.shape)
    route = jnp.zeros(logits.shape, F32)
    for col, val in enumerate((i1.astype(F32), i2.astype(F32), 1.0 / den, e2 / den, rank1, rank2)):
        route = jnp.where(lane == col, val, route)
    route_ref[...] = route


def _outproj(att, gm, x, w, g, router_pad):
    T = x.shape[0]
    tm = _tile(T, 512)
    moe = router_pad is not None
    full = lambda i: (0, 0)
    row = lambda i: (i, 0)
    in_specs = [
        pl.BlockSpec((tm, ATTN_WIDTH), row),
        pl.BlockSpec((tm, GMLP_WIDTH), row),
        pl.BlockSpec((tm, D_MODEL), row),
        pl.BlockSpec(w.shape, full),
        pl.BlockSpec((1, D_MODEL), full),
    ]
    args = [att, gm, x, w, g]
    if not moe:
        out_specs = [pl.BlockSpec((tm, D_MODEL), row), pl.BlockSpec((tm, D_MODEL), row)]
        out_shape = [jax.ShapeDtypeStruct((T, D_MODEL), F32),
                     jax.ShapeDtypeStruct((T, D_MODEL), BF16)]
        scratch = []
    else:
        in_specs.append(pl.BlockSpec(router_pad.shape, full))
        args.append(router_pad)
        out_specs = [pl.BlockSpec((tm, D_MODEL), row), pl.BlockSpec((tm, D_MODEL // 2), row),
                     pl.BlockSpec((tm, LANES), row), pl.BlockSpec((8, LANES), full)]
        out_shape = [jax.ShapeDtypeStruct((T, D_MODEL), F32),
                     jax.ShapeDtypeStruct((T, D_MODEL // 2), jnp.uint32),
                     jax.ShapeDtypeStruct((T, LANES), F32),
                     jax.ShapeDtypeStruct((8, LANES), F32)]
        scratch = [pltpu.VMEM((1, LANES), F32)]
    return pl.pallas_call(
        functools.partial(_outproj_kernel, moe=moe),
        grid=(T // tm,),
        in_specs=in_specs,
        out_specs=out_specs,
        out_shape=out_shape,
        scratch_shapes=scratch,
        compiler_params=_params(("arbitrary",) if moe else ("parallel",)),
        name="outproj_moe" if moe else "outproj",
    )(*args)


def _sc_scatter_rows(src, dst2, n_rows):
    n, width = src.shape
    info = plsc.get_sparse_core_info()
    n_workers = info.num_cores * info.num_subcores
    per_worker = n // n_workers
    win = SC_SCATTER_WINDOW
    idx = [dst2[:, k].reshape(n // win, win) for k in range(TOP_K)]
    mesh = plsc.VectorSubcoreMesh(core_axis_name="c", subcore_axis_name="s")

    @functools.partial(
        pl.kernel, mesh=mesh,
        out_type=jax.ShapeDtypeStruct((n_rows, width), src.dtype),
        scratch_types=[pltpu.VMEM((1, win), jnp.int32), pltpu.VMEM((1, win), jnp.int32),
                       pltpu.VMEM((win, width), src.dtype)])
    def scatter(src_hbm, i0_hbm, i1_hbm, out_hbm, i0_v, i1_v, rows_v):
        wid = lax.axis_index("s") * info.num_cores + lax.axis_index("c")
        base = wid * per_worker

        @pl.loop(0, per_worker, step=win)
        def _(off):
            w = (base + off) // win
            pltpu.sync_copy(i0_hbm.at[pl.ds(w, 1)], i0_v)
            pltpu.sync_copy(i1_hbm.at[pl.ds(w, 1)], i1_v)
            pltpu.sync_copy(src_hbm.at[pl.ds(base + off, win)], rows_v)
            pltpu.sync_copy(rows_v, out_hbm.at[i0_v.at[0]])
            pltpu.sync_copy(rows_v, out_hbm.at[i1_v.at[0]])

    return scatter(src, *idx)


def _swiglu_step(h_ref, wg_ref, wu_ref, wd_ref, acc_ref):
    tm = h_ref.shape[0]
    rb = tm // FFN_ROW_SPLIT

    @pl.when(pl.program_id(1) == 0)
    def _():
        acc_ref[...] = jnp.zeros(acc_ref.shape, F32)

    for s in range(FFN_ROW_SPLIT):
        rows = slice(s * rb, (s + 1) * rb)
        h = h_ref[rows, :]
        a = (jax.nn.silu(jnp.dot(h, wg_ref[...], preferred_element_type=F32))
             * jnp.dot(h, wu_ref[...], preferred_element_type=F32)).astype(BF16)
        acc_ref[rows, :] += jnp.dot(a, wd_ref[...], preferred_element_type=F32)


def _dense_ffn_kernel(h_ref, x_ref, wg_ref, wu_ref, wd_ref, o_ref, acc_ref):
    _swiglu_step(h_ref, wg_ref, wu_ref, wd_ref, acc_ref)

    @pl.when(pl.program_id(1) == pl.num_programs(1) - 1)
    def _():
        o_ref[...] = x_ref[...] + acc_ref[...]


def _dense_ffn(h, x, wg, wu, wd):
    T = h.shape[0]
    tm = _tile(T, FFN_TM)
    tf = D_FF // 2 if D_FF % 512 == 0 else _tile(D_FF, 512)
    row = lambda i, j: (i, 0)
    return pl.pallas_call(
        _dense_ffn_kernel,
        grid=(T // tm, D_FF // tf),
        in_specs=[
            pl.BlockSpec((tm, D_MODEL), row),
            pl.BlockSpec((tm, D_MODEL), row),
            pl.BlockSpec((D_MODEL, tf), lambda i, j: (0, j)),
            pl.BlockSpec((D_MODEL, tf), lambda i, j: (0, j)),
            pl.BlockSpec((tf, D_MODEL), lambda i, j: (j, 0)),
        ],
        out_specs=pl.BlockSpec((tm, D_MODEL), row),
        out_shape=jax.ShapeDtypeStruct((T, D_MODEL), F32),
        scratch_shapes=[pltpu.VMEM((tm, D_MODEL), F32)],
        compiler_params=_params(("parallel", "arbitrary")),
        name="dense_ffn",
    )(h, x, wg, wu, wd)


def _moe_ffn_kernel(te_ref, nv_ref, xp_ref, wg_ref, wu_ref, wd_ref, o_ref, acc_ref, h_ref):
    n_valid = nv_ref[pl.program_id(0)]
    last = pl.program_id(1) == pl.num_programs(1) - 1

    @pl.when(n_valid > 0)
    def _():
        @pl.when(pl.program_id(1) == 0)
        def _():
            lo, hi = _unpack_bf16_pairs(xp_ref[...])
            live = lax.broadcasted_iota(jnp.int32, lo.shape, 0) < n_valid
            h_ref[:, :D_MODEL // 2] = jnp.where(live, lo, jnp.zeros_like(lo))
            h_ref[:, D_MODEL // 2:] = jnp.where(live, hi, jnp.zeros_like(hi))

        _swiglu_step(h_ref, wg_ref, wu_ref, wd_ref, acc_ref)

        @pl.when(last)
        def _():
            o_ref[...] = acc_ref[...]

    @pl.when(jnp.logical_and(n_valid == 0, last))
    def _():
        o_ref[...] = jnp.zeros(o_ref.shape, o_ref.dtype)


def _moe_ffn(xs, tile_expert, n_valid, wg, wu, wd, tm):
    P = xs.shape[0]
    tf = D_FF // 2 if D_FF % 512 == 0 else _tile(D_FF, 512)
    row = lambda i, j, te, nu: (i, 0)
    return pl.pallas_call(
        _moe_ffn_kernel,
        grid_spec=pltpu.PrefetchScalarGridSpec(
            num_scalar_prefetch=2,
            grid=(P // tm, D_FF // tf),
            in_specs=[
                pl.BlockSpec((tm, D_MODEL // 2), row),
                pl.BlockSpec((None, D_MODEL, tf), lambda i, j, te, nu: (te[i], 0, j)),
                pl.BlockSpec((None, D_MODEL, tf), lambda i, j, te, nu: (te[i], 0, j)),
                pl.BlockSpec((None, tf, D_MODEL), lambda i, j, te, nu: (te[i], j, 0)),
            ],
            out_specs=pl.BlockSpec((tm, D_MODEL), row),
            scratch_shapes=[pltpu.VMEM((tm, D_MODEL), F32), pltpu.VMEM((tm, D_MODEL), BF16)],
        ),
        out_shape=jax.ShapeDtypeStruct((P, D_MODEL), F32),
        compiler_params=_params(("arbitrary", "arbitrary")),
        name="moe_ffn",
    )(tile_expert, n_valid, xs, wg, wu, wd)


def _sc_gather_rows(src, idx):
    n = idx.shape[0]
    info = plsc.get_sparse_core_info()
    n_workers = info.num_cores * info.num_subcores
    per_worker = n // n_workers
    mesh = plsc.VectorSubcoreMesh(core_axis_name="c", subcore_axis_name="s")

    @functools.partial(
        pl.kernel, mesh=mesh,
        out_type=jax.ShapeDtypeStruct((n, D_MODEL), src.dtype),
        scratch_types=[pltpu.VMEM((SC_WINDOW,), jnp.int32),
                       pltpu.VMEM((SC_WINDOW, D_MODEL), src.dtype),
                       pltpu.SemaphoreType.DMA])
    def gather(src_hbm, idx_hbm, out_hbm, idx_v, rows_v, sem):
        wid = lax.axis_index("s") * info.num_cores + lax.axis_index("c")
        base = wid * per_worker

        @pl.loop(0, per_worker, step=SC_WINDOW)
        def _(off):
            pltpu.sync_copy(idx_hbm.at[pl.ds(base + off, SC_WINDOW)], idx_v)
            pltpu.async_copy(src_hbm.at[idx_v], rows_v, sem).wait()
            pltpu.sync_copy(rows_v, out_hbm.at[pl.ds(base + off, SC_WINDOW)])

    return gather(src, idx)


def _ple_kernel(*refs, moe, last):
    refs = list(refs)
    x_ref = refs.pop(0)
    if moe:
        y0_ref, y1_ref, route_ref = refs.pop(0), refs.pop(0), refs.pop(0)
    p_ref, wg_ref, b_ref, pw_ref, pn_ref = refs[:5]
    refs = refs[5:]
    if last:
        fn_ref = refs.pop(0)
    o_ref = refs.pop(0)

    e = jnp.dot(p_ref[...].astype(BF16), pw_ref[...], preferred_element_type=F32)
    e = _rms(e) * pn_ref[...]
    x = x_ref[...]
    if moe:
        rt = route_ref[...]
        x = x + (rt[:, 2:3] * y0_ref[...] + rt[:, 3:4] * y1_ref[...])
    gate = jax.nn.sigmoid(jnp.dot(_rms(x).astype(BF16), wg_ref[...], preferred_element_type=F32)
                          + b_ref[...])
    x = x + gate * e
    if last:
        x = _rms(x) * fn_ref[...]
    o_ref[...] = x


def _ple(x, dst, ys, route, p, layer, wg, b, pw, pn, final_norm):
    T = x.shape[0]
    tm = _tile(T, 512)
    moe = ys is not None
    last = final_norm is not None
    full = lambda i: (0, 0)
    row = lambda i: (i, 0)
    in_specs = [pl.BlockSpec((tm, D_MODEL), row)]
    args = [x]
    scratch = []
    if moe:
        in_specs += [pl.BlockSpec((tm, D_MODEL), row), pl.BlockSpec((tm, D_MODEL), row),
                     pl.BlockSpec((tm, LANES), row)]
        args += [_sc_gather_rows(ys, dst[:, 0]), _sc_gather_rows(ys, dst[:, 1]), route]
    in_specs += [
        pl.BlockSpec((None, tm, PLE_DIM), lambda i: (layer, i, 0)),
        pl.BlockSpec(wg.shape, full),
        pl.BlockSpec((1, D_MODEL), full),
        pl.BlockSpec(pw.shape, full),
        pl.BlockSpec((1, D_MODEL), full),
    ]
    args += [p, wg, b, pw, pn]
    if last:
        in_specs.append(pl.BlockSpec((1, D_MODEL), full))
        args.append(final_norm)
    return pl.pallas_call(
        functools.partial(_ple_kernel, moe=moe, last=last),
        grid=(T // tm,),
        in_specs=in_specs,
        out_specs=pl.BlockSpec((tm, D_MODEL), row),
        out_shape=jax.ShapeDtypeStruct((T, D_MODEL), F32),
        scratch_shapes=scratch,
        compiler_params=_params(("parallel",)),
        name="ple" + ("_moe" if moe else "") + ("_last" if last else ""),
    )(*args)


def _rope_tables(seq):
    half = ROT_DIM // 2
    pos = jnp.arange(seq, dtype=F32)
    inv = ROPE_THETA ** (-jnp.arange(half, dtype=F32) * 2.0 / ROT_DIM)
    ang = pos[:, None] * inv[None, :]
    cos, sin = jnp.cos(ang), jnp.sin(ang)
    ones = jnp.ones((seq, HEAD_DIM - ROT_DIM), F32)
    zeros = jnp.zeros((seq, HEAD_DIM - ROT_DIM), F32)
    zh = jnp.zeros((seq, half), F32)
    c = jnp.concatenate([cos, cos, ones], axis=1)
    sa = jnp.concatenate([zh, sin, zeros], axis=1)
    sb = jnp.concatenate([-sin, zh, zeros], axis=1)
    rep = lambda t: jnp.concatenate([t, t], axis=1)
    return rep(c), rep(sa), rep(sb)


def _group_layout(route, counts, tm):
    T = route.shape[0]
    counts = counts[0, :N_EXPERTS].astype(jnp.int32)
    n_tile_e = (counts + tm - 1) // tm
    tile_end = jnp.cumsum(n_tile_e)
    start = (tile_end - n_tile_e) * tm
    expert = route[:, 0:TOP_K].astype(jnp.int32)
    rank = route[:, 4:4 + TOP_K].astype(jnp.int32)
    onehot = expert[..., None] == jnp.arange(N_EXPERTS, dtype=jnp.int32)
    dst = jnp.sum(jnp.where(onehot, start, 0), axis=-1) + rank
    n_tiles = TOP_K * T // tm + N_EXPERTS
    tile = jnp.arange(n_tiles, dtype=jnp.int32)
    tile_expert = jnp.minimum(
        jnp.sum((tile[:, None] >= tile_end[None, :]).astype(jnp.int32), axis=1), N_EXPERTS - 1)
    own = tile_expert[:, None] == jnp.arange(N_EXPERTS, dtype=jnp.int32)[None, :]
    left = jnp.sum(jnp.where(own, counts - (tile[:, None] - (tile_end - n_tile_e)) * tm, 0), axis=1)
    n_valid = jnp.where(tile < tile_end[-1], jnp.clip(left, 0, tm), 0).astype(jnp.int32)
    return dst, tile_expert, n_valid, n_tiles * tm


def _trunk(x, p, W):
    B, S, _ = x.shape
    T = B * S
    depth = p.shape[0]
    x = x.reshape(T, D_MODEL)
    p = p.reshape(depth, T, PLE_DIM)
    cos, sa, sb = _rope_tables(S)
    for i in range(depth):
        lam_init = 0.8 - 0.6 * math.exp(-0.3 * i)
        q, k, v, gm = _inproj(x, W["norm_mix"][i], W["w_in"][i], cos, sa, sb, W["gmlp_ln_g"][i],
                              W["gmlp_ln_b"][i], W["gmlp_ws"][i], W["gmlp_bs"][i], S)
        att = _attention(q, k, v, W["lam"][i], W["subln"][i], B, S, lam_init)
        moe = i % 2 == 1
        j = i // 2
        fin = W["final_norm"] if i == depth - 1 else None
        ple_w = (W["ple_gate_w"][i], W["ple_gate_b"][i], W["ple_w"][i], W["ple_norm"][i], fin)
        if not moe:
            x1, h2 = _outproj(att, gm, x, W["w_out"][i], W["norm_ffn"][i], None)
            x2 = _dense_ffn(h2, x1, W["dense_w_gate"][j], W["dense_w_up"][j], W["dense_w_down"][j])
            x = _ple(x2, None, None, None, p, i, *ple_w)
        else:
            x1, hp, route, counts = _outproj(att, gm, x, W["w_out"][i], W["norm_ffn"][i],
                                             W["router"][j])
            tm = _tile(T, MOE_TM)
            dst, tile_expert, n_valid, n_rows = _group_layout(route, counts, tm)
            xs = _sc_scatter_rows(hp, dst, n_rows)
            ys = _moe_ffn(xs, tile_expert, n_valid, W["moe_w_gate"][j], W["moe_w_up"][j],
                          W["moe_w_down"][j], tm)
            x = _ple(x1, dst, ys, route, p, i, *ple_w)
    return x.reshape(B, S, D_MODEL)


def kernel(x_prompt, x_sample, p_prompt, p_sample, norm_mix, w_in, lambda_q1, lambda_k1, lambda_q2, lambda_k2, subln, gmlp_ln_g, gmlp_ln_b, gmlp_ws, gmlp_bs, w_out, norm_ffn, dense_w_gate, dense_w_up, dense_w_down, router, moe_w_gate, moe_w_up, moe_w_down, ple_w, ple_norm, ple_gate_w, ple_gate_b, final_norm):
    depth = w_in.shape[0]
    row = lambda t: t[:, None, :]
    lam = jnp.stack([lambda_q1, lambda_k1, lambda_q2, lambda_k2], axis=1)
    lam = jnp.pad(lam, ((0, 0), (0, 4), (0, LANES - HEAD_DIM)))
    r_hi = router.astype(BF16)
    r_lo = (router - r_hi.astype(F32)).astype(BF16)
    W = dict(
        norm_mix=row(norm_mix), w_in=w_in.astype(BF16), lam=lam, subln=row(subln),
        gmlp_ln_g=row(gmlp_ln_g), gmlp_ln_b=row(gmlp_ln_b), gmlp_ws=gmlp_ws.astype(BF16),
        gmlp_bs=jnp.broadcast_to(gmlp_bs[..., None], gmlp_bs.shape + (CHUNK,)),
        w_out=w_out.astype(BF16), norm_ffn=row(norm_ffn),
        dense_w_gate=dense_w_gate.astype(BF16), dense_w_up=dense_w_up.astype(BF16),
        dense_w_down=dense_w_down.astype(BF16),
        router=jnp.pad(jnp.concatenate([r_hi, r_lo], axis=-1),
                       ((0, 0), (0, 0), (0, LANES - 2 * N_EXPERTS))),
        moe_w_gate=moe_w_gate.astype(BF16), moe_w_up=moe_w_up.astype(BF16),
        moe_w_down=moe_w_down.astype(BF16),
        ple_w=ple_w.astype(BF16), ple_norm=row(ple_norm), ple_gate_w=ple_gate_w.astype(BF16),
        ple_gate_b=row(ple_gate_b), final_norm=final_norm[None, :],
    )
    return (_trunk(x_prompt, p_prompt, W), _trunk(x_sample, p_sample, W))
```

```python
import functools
import math

import jax
import jax.numpy as jnp
from jax import lax
from jax.experimental import pallas as pl
from jax.experimental.pallas import tpu as pltpu
from jax.experimental.pallas import tpu_sc as plsc

F32 = jnp.float32
BF16 = jnp.bfloat16

D_MODEL = 1024
ATTN_WIDTH = 512
GMLP_WIDTH = 512
HEAD_DIM = 64
HEAD_WIDTH = 2 * HEAD_DIM
N_HEADS = ATTN_WIDTH // HEAD_WIDTH
ROT_DIM = HEAD_DIM // 4
ROPE_THETA = 500000.0
CHUNK = 128
N_GROUPS = 4
D_FF = 3584
N_EXPERTS = 8
TOP_K = 2
PLE_DIM = 256
EPS = 1e-6
LANES = 128
Q_SCALE = (1.0 / math.sqrt(HEAD_DIM)) * math.log2(math.e)
VMEM_LIMIT = 48 * 1024 * 1024
ATTN_TQ = 1024
ATTN_TK = 1024
ATTN_COLS = 256
FFN_TM = 512
MOE_TM = 512
FFN_ROW_SPLIT = 2
SC_WINDOW = 32
SC_SCATTER_WINDOW = 128
ROW_TM = 1024
ROW_SPLIT = 1


def _tile(n, pref):
    t = min(n, pref)
    while n % t:
        t //= 2
    return t


def _params(sem, **kw):
    return pltpu.CompilerParams(dimension_semantics=sem, vmem_limit_bytes=VMEM_LIMIT, **kw)


def _rms(x):
    return x * lax.rsqrt(jnp.mean(x * x, axis=-1, keepdims=True) + EPS)


def _inproj_kernel(x_ref, g_ref, w_ref, cos_ref, sa_ref, sb_ref, lng_ref, lnb_ref, ws_ref, bs_ref,
                   q_ref, k_ref, v_ref, gm_ref):
    tm = x_ref.shape[0]
    rb = tm // ROW_SPLIT
    half = ROT_DIM // 2
    for s in range(ROW_SPLIT):
        rows = slice(s * rb, (s + 1) * rb)
        h = (_rms(x_ref[rows, :]) * g_ref[...]).astype(BF16)

        def proj(c0, n):
            return jnp.dot(h, w_ref[:, c0:c0 + n], preferred_element_type=F32)

        cos, sa, sb = cos_ref[rows, :], sa_ref[rows, :], sb_ref[rows, :]

        def rope(t):
            outs = []
            for hh in range(N_HEADS):
                th = t[:, hh * HEAD_WIDTH:(hh + 1) * HEAD_WIDTH]
                outs.append(th * cos + pltpu.roll(th, half, 1) * sa
                            + pltpu.roll(th, HEAD_WIDTH - half, 1) * sb)
            return jnp.concatenate(outs, axis=1)

        q_ref[rows, :] = (rope(proj(0, ATTN_WIDTH)) * Q_SCALE).astype(BF16)
        k_ref[rows, :] = rope(proj(ATTN_WIDTH, ATTN_WIDTH)).astype(BF16)
        v_ref[rows, :] = proj(2 * ATTN_WIDTH, ATTN_WIDTH).astype(BF16)

        u = jax.nn.gelu(proj(3 * ATTN_WIDTH, GMLP_WIDTH))
        vf = jax.nn.gelu(proj(3 * ATTN_WIDTH + GMLP_WIDTH, GMLP_WIDTH))
        mu = jnp.mean(vf, axis=-1, keepdims=True)
        vc = vf - mu
        var = jnp.mean(vc * vc, axis=-1, keepdims=True)
        vn = (vc * lax.rsqrt(var + EPS) * lng_ref[...] + lnb_ref[...]).astype(BF16)
        for c in range(rb // CHUNK):
            r = slice(c * CHUNK, (c + 1) * CHUNK)
            ro = slice(s * rb + c * CHUNK, s * rb + (c + 1) * CHUNK)
            for g in range(N_GROUPS):
                cg = slice(g * CHUNK, (g + 1) * CHUNK)
                mixed = jnp.dot(ws_ref[g], vn[r, cg], preferred_element_type=F32) + bs_ref[g]
                gm_ref[ro, cg] = (u[r, cg] * mixed).astype(BF16)


def _inproj(x, g, w, cos, sa, sb, lng, lnb, ws, bsb, seq):
    T = x.shape[0]
    tm = _tile(seq, ROW_TM)
    ns = seq // tm
    full = lambda i: (0, 0)
    out = jax.ShapeDtypeStruct((T, ATTN_WIDTH), BF16)
    return pl.pallas_call(
        _inproj_kernel,
        grid=(T // tm,),
        in_specs=[
            pl.BlockSpec((tm, D_MODEL), lambda i: (i, 0)),
            pl.BlockSpec((1, D_MODEL), full),
            pl.BlockSpec(w.shape, full),
            pl.BlockSpec((tm, HEAD_WIDTH), lambda i: (i % ns, 0)),
            pl.BlockSpec((tm, HEAD_WIDTH), lambda i: (i % ns, 0)),
            pl.BlockSpec((tm, HEAD_WIDTH), lambda i: (i % ns, 0)),
            pl.BlockSpec((1, GMLP_WIDTH), full),
            pl.BlockSpec((1, GMLP_WIDTH), full),
            pl.BlockSpec(ws.shape, lambda i: (0, 0, 0)),
            pl.BlockSpec(bsb.shape, lambda i: (0, 0, 0)),
        ],
        out_specs=[pl.BlockSpec((tm, ATTN_WIDTH), lambda i: (i, 0))] * 4,
        out_shape=[out] * 4,
        compiler_params=_params(("parallel",)),
        name="inproj",
    )(x, g, w, cos, sa, sb, lng, lnb, ws, bsb)


def _attn_kernel(q_ref, k_ref, v_ref, lam_ref, sub_ref, o_ref,
                 qcat_ref, sa_ref, sb_ref, m_ref, l_ref, acc_ref, *, tk, lam_init):
    tq = q_ref.shape[0]
    n = k_ref.shape[0] // tk
    qt = q_ref[...].astype(F32).T
    row = lax.broadcasted_iota(jnp.int32, qt.shape, 0)
    qcat_ref[:, :tq] = jnp.where(row < HEAD_DIM, qt, 0.0).astype(BF16)
    qcat_ref[:, tq:] = jnp.where(row >= HEAD_DIM, qt, 0.0).astype(BF16)
    m_ref[...] = jnp.full(m_ref.shape, -jnp.inf, F32)
    l_ref[...] = jnp.zeros(l_ref.shape, F32)
    acc_ref[...] = jnp.zeros(acc_ref.shape, F32)

    def rows(t):
        return pl.ds(pl.multiple_of(t * tk, tk), tk)

    def scores(t, s_ref):
        s_ref[...] = jnp.dot(k_ref[rows(t), :], qcat_ref[...], preferred_element_type=F32)

    def update(t, s_ref):
        vt = v_ref[rows(t), :]
        for cb in range(2 * tq // ATTN_COLS):
            cols = slice(cb * ATTN_COLS, (cb + 1) * ATTN_COLS)
            s = s_ref[:, cols]
            m_old = m_ref[:, cols]
            m_new = jnp.maximum(m_old, jnp.max(s, axis=0, keepdims=True))
            alpha = jnp.exp2(m_old - m_new)
            p = jnp.exp2(s - m_new)
            l_ref[:, cols] = alpha * l_ref[:, cols] + jnp.sum(p, axis=0, keepdims=True)
            m_ref[:, cols] = m_new
            pv = lax.dot_general(vt, p.astype(BF16), (((0,), (0,)), ((), ())),
                                 preferred_element_type=F32)
            acc_ref[:, cols] = alpha * acc_ref[:, cols] + pv

    scores(0, sa_ref)

    def pair(u, carry):
        t = 2 * u
        scores(t + 1, sb_ref)
        update(t, sa_ref)
        scores(t + 2, sa_ref)
        update(t + 1, sb_ref)
        return carry

    lax.fori_loop(0, n // 2 - 1, pair, 0)
    scores(n - 1, sb_ref)
    update(n - 2, sa_ref)
    update(n - 1, sb_ref)

    lp = lam_ref[...]
    lam = (jnp.exp(jnp.sum(lp[0:1] * lp[1:2], axis=-1, keepdims=True))
           - jnp.exp(jnp.sum(lp[2:3] * lp[3:4], axis=-1, keepdims=True)) + lam_init)
    l = l_ref[...]
    ot = acc_ref[:, :tq] / l[:, :tq] - lam * (acc_ref[:, tq:] / l[:, tq:])
    o = ot.T
    o_ref[...] = (_rms(o) * sub_ref[...] * (1.0 - lam_init)).astype(o_ref.dtype)


def _attention(q, k, v, lam_p, sub, batch, seq, lam_init):
    T = q.shape[0]
    tq = _tile(seq, ATTN_TQ)
    tk = _tile(seq // 2, ATTN_TK)
    nq = seq // tq
    kv_spec = pl.BlockSpec((seq, HEAD_WIDTH), lambda b, h, i: (b, h))
    return pl.pallas_call(
        functools.partial(_attn_kernel, tk=tk, lam_init=lam_init),
        grid=(batch, N_HEADS, nq),
        in_specs=[
            pl.BlockSpec((tq, HEAD_WIDTH), lambda b, h, i: (b * nq + i, h)),
            kv_spec,
            kv_spec,
            pl.BlockSpec(lam_p.shape, lambda b, h, i: (0, 0)),
            pl.BlockSpec((1, HEAD_WIDTH), lambda b, h, i: (0, 0)),
        ],
        out_specs=pl.BlockSpec((tq, HEAD_WIDTH), lambda b, h, i: (b * nq + i, h)),
        out_shape=jax.ShapeDtypeStruct((T, ATTN_WIDTH), BF16),
        scratch_shapes=[
            pltpu.VMEM((HEAD_WIDTH, 2 * tq), BF16),
            pltpu.VMEM((tk, 2 * tq), F32),
            pltpu.VMEM((tk, 2 * tq), F32),
            pltpu.VMEM((1, 2 * tq), F32),
            pltpu.VMEM((1, 2 * tq), F32),
            pltpu.VMEM((HEAD_WIDTH, 2 * tq), F32),
        ],
        compiler_params=_params(("parallel", "parallel", "parallel")),
        name="attn",
    )(q, k, v, lam_p, sub)


def _pack_bf16_pairs(h):
    half = D_MODEL // 2
    h16 = h.astype(BF16)
    lo = lax.bitcast_convert_type(h16[:, :half].astype(F32), jnp.uint32)
    hi = lax.bitcast_convert_type(h16[:, half:].astype(F32), jnp.uint32)
    return (lo >> 16) | (hi & jnp.uint32(0xFFFF0000))


def _unpack_bf16_pairs(xp):
    lo = lax.bitcast_convert_type(xp << 16, F32)
    hi = lax.bitcast_convert_type(xp & jnp.uint32(0xFFFF0000), F32)
    return lo.astype(BF16), hi.astype(BF16)


def _outproj_kernel(att_ref, gm_ref, x_ref, w_ref, g_ref, *rest, moe):
    if moe:
        r_ref, x1_ref, h_ref, route_ref, count_ref, cnt_ref = rest
    else:
        x1_ref, h_ref = rest
    y = (jnp.dot(att_ref[...], w_ref[:ATTN_WIDTH, :], preferred_element_type=F32)
         + jnp.dot(gm_ref[...], w_ref[ATTN_WIDTH:, :], preferred_element_type=F32))
    x1 = x_ref[...] + y
    x1_ref[...] = x1
    h = _rms(x1) * g_ref[...]
    if not moe:
        h_ref[...] = h.astype(BF16)
        return
    h_ref[...] = _pack_bf16_pairs(h)
    tm = h.shape[0]
    h_hi = h.astype(BF16)
    h_lo = (h - h_hi.astype(F32)).astype(BF16)
    part = (jnp.dot(h_hi, r_ref[...], preferred_element_type=F32)
            + jnp.dot(h_lo, r_ref[...], preferred_element_type=F32))
    logits = part + pltpu.roll(part, LANES - N_EXPERTS, 1)
    lane = lax.broadcasted_iota(jnp.int32, logits.shape, 1)
    lg = jnp.where(lane < N_EXPERTS, logits, -jnp.inf)
    v1 = jnp.max(lg, axis=-1, keepdims=True)
    i1 = jnp.min(jnp.where(lg == v1, lane, LANES), axis=-1, keepdims=True)
    lg2 = jnp.where(lane == i1, -jnp.inf, lg)
    v2 = jnp.max(lg2, axis=-1, keepdims=True)
    i2 = jnp.min(jnp.where(lg2 == v2, lane, LANES), axis=-1, keepdims=True)
    e2 = jnp.exp(v2 - v1)
    den = 1.0 + e2

    @pl.when(pl.program_id(0) == 0)
    def _():
        cnt_ref[...] = jnp.zeros(cnt_ref.shape, F32)

    sel1 = lane == i1
    sel2 = lane == i2
    onehot = jnp.where(jnp.logical_or(sel1, sel2), 1.0, 0.0)
    r_i = lax.broadcasted_iota(jnp.int32, (tm, tm), 0)
    c_i = lax.broadcasted_iota(jnp.int32, (tm, tm), 1)
    tri = jnp.where(c_i < r_i, 1.0, 0.0).astype(BF16)
    before = jnp.dot(tri, onehot.astype(BF16), preferred_element_type=F32) + cnt_ref[...]
    rank1 = jnp.sum(jnp.where(sel1, before, 0.0), axis=-1, keepdims=True)
    rank2 = jnp.sum(jnp.where(sel2, before, 0.0), axis=-1, keepdims=True)
    cnt = cnt_ref[...] + jnp.sum(onehot, axis=0, keepdims=True)
    cnt_ref[...] = cnt
    count_ref[...] = jnp.broadcast_to(cnt, count_ref.shape)
    route = jnp.zeros(logits.shape, F32)
    for col, val in enumerate((i1.astype(F32), i2.astype(F32), 1.0 / den, e2 / den, rank1, rank2)):
        route = jnp.where(lane == col, val, route)
    route_ref[...] = route


def _outproj(att, gm, x, w, g, router_pad):
    T = x.shape[0]
    tm = _tile(T, 512)
    moe = router_pad is not None
    full = lambda i: (0, 0)
    row = lambda i: (i, 0)
    in_specs = [
        pl.BlockSpec((tm, ATTN_WIDTH), row),
        pl.BlockSpec((tm, GMLP_WIDTH), row),
        pl.BlockSpec((tm, D_MODEL), row),
        pl.BlockSpec(w.shape, full),
        pl.BlockSpec((1, D_MODEL), full),
    ]
    args = [att, gm, x, w, g]
    if not moe:
        out_specs = [pl.BlockSpec((tm, D_MODEL), row), pl.BlockSpec((tm, D_MODEL), row)]
        out_shape = [jax.ShapeDtypeStruct((T, D_MODEL), F32),
                     jax.ShapeDtypeStruct((T, D_MODEL), BF16)]
        scratch = []
    else:
        in_specs.append(pl.BlockSpec(router_pad.shape, full))
        args.append(router_pad)
        out_specs = [pl.BlockSpec((tm, D_MODEL), row), pl.BlockSpec((tm, D_MODEL // 2), row),
                     pl.BlockSpec((tm, LANES), row), pl.BlockSpec((8, LANES), full)]
        out_shape = [jax.ShapeDtypeStruct((T, D_MODEL), F32),
                     jax.ShapeDtypeStruct((T, D_MODEL // 2), jnp.uint32),
                     jax.ShapeDtypeStruct((T, LANES), F32),
                     jax.ShapeDtypeStruct((8, LANES), F32)]
        scratch = [pltpu.VMEM((1, LANES), F32)]
    return pl.pallas_call(
        functools.partial(_outproj_kernel, moe=moe),
        grid=(T // tm,),
        in_specs=in_specs,
        out_specs=out_specs,
        out_shape=out_shape,
        scratch_shapes=scratch,
        compiler_params=_params(("arbitrary",) if moe else ("parallel",)),
        name="outproj_moe" if moe else "outproj",
    )(*args)


def _sc_scatter_rows(src, dst2, n_rows):
    n, width = src.shape
    info = plsc.get_sparse_core_info()
    n_workers = info.num_cores * info.num_subcores
    per_worker = n // n_workers
    win = SC_SCATTER_WINDOW
    idx = [dst2[:, k].reshape(n // win, win) for k in range(TOP_K)]
    mesh = plsc.VectorSubcoreMesh(core_axis_name="c", subcore_axis_name="s")

    @functools.partial(
        pl.kernel, mesh=mesh,
        out_type=jax.ShapeDtypeStruct((n_rows, width), src.dtype),
        scratch_types=[pltpu.VMEM((1, win), jnp.int32), pltpu.VMEM((1, win), jnp.int32),
                       pltpu.VMEM((win, width), src.dtype)])
    def scatter(src_hbm, i0_hbm, i1_hbm, out_hbm, i0_v, i1_v, rows_v):
        wid = lax.axis_index("s") * info.num_cores + lax.axis_index("c")
        base = wid * per_worker

        @pl.loop(0, per_worker, step=win)
        def _(off):
            w = (base + off) // win
            pltpu.sync_copy(i0_hbm.at[pl.ds(w, 1)], i0_v)
            pltpu.sync_copy(i1_hbm.at[pl.ds(w, 1)], i1_v)
            pltpu.sync_copy(src_hbm.at[pl.ds(base + off, win)], rows_v)
            pltpu.sync_copy(rows_v, out_hbm.at[i0_v.at[0]])
            pltpu.sync_copy(rows_v, out_hbm.at[i1_v.at[0]])

    return scatter(src, *idx)


def _swiglu_step(h_ref, wg_ref, wu_ref, wd_ref, acc_ref):
    tm = h_ref.shape[0]
    rb = tm // FFN_ROW_SPLIT

    @pl.when(pl.program_id(1) == 0)
    def _():
        acc_ref[...] = jnp.zeros(acc_ref.shape, F32)

    for s in range(FFN_ROW_SPLIT):
        rows = slice(s * rb, (s + 1) * rb)
        h = h_ref[rows, :]
        a = (jax.nn.silu(jnp.dot(h, wg_ref[...], preferred_element_type=F32))
             * jnp.dot(h, wu_ref[...], preferred_element_type=F32)).astype(BF16)
        acc_ref[rows, :] += jnp.dot(a, wd_ref[...], preferred_element_type=F32)


def _dense_ffn_kernel(h_ref, x_ref, wg_ref, wu_ref, wd_ref, o_ref, acc_ref):
    _swiglu_step(h_ref, wg_ref, wu_ref, wd_ref, acc_ref)

    @pl.when(pl.program_id(1) == pl.num_programs(1) - 1)
    def _():
        o_ref[...] = x_ref[...] + acc_ref[...]


def _dense_ffn(h, x, wg, wu, wd, layer):
    T = h.shape[0]
    tm = _tile(T, FFN_TM)
    tf = D_FF // 2 if D_FF % 512 == 0 else _tile(D_FF, 512)
    row = lambda i, j: (i, 0)
    return pl.pallas_call(
        _dense_ffn_kernel,
        grid=(T // tm, D_FF // tf),
        in_specs=[
            pl.BlockSpec((tm, D_MODEL), row),
            pl.BlockSpec((tm, D_MODEL), row),
            pl.BlockSpec((None, D_MODEL, tf), lambda i, j: (layer, 0, j)),
            pl.BlockSpec((None, D_MODEL, tf), lambda i, j: (layer, 0, j)),
            pl.BlockSpec((None, tf, D_MODEL), lambda i, j: (layer, j, 0)),
        ],
        out_specs=pl.BlockSpec((tm, D_MODEL), row),
        out_shape=jax.ShapeDtypeStruct((T, D_MODEL), F32),
        scratch_shapes=[pltpu.VMEM((tm, D_MODEL), F32)],
        compiler_params=_params(("parallel", "arbitrary")),
        name="dense_ffn",
    )(h, x, wg, wu, wd)


def _moe_ffn_kernel(te_ref, nv_ref, xp_ref, wg_ref, wu_ref, wd_ref, o_ref, acc_ref, h_ref):
    n_valid = nv_ref[pl.program_id(0)]
    last = pl.program_id(1) == pl.num_programs(1) - 1

    @pl.when(n_valid > 0)
    def _():
        @pl.when(pl.program_id(1) == 0)
        def _():
            lo, hi = _unpack_bf16_pairs(xp_ref[...])
            live = lax.broadcasted_iota(jnp.int32, lo.shape, 0) < n_valid
            h_ref[:, :D_MODEL // 2] = jnp.where(live, lo, jnp.zeros_like(lo))
            h_ref[:, D_MODEL // 2:] = jnp.where(live, hi, jnp.zeros_like(hi))

        _swiglu_step(h_ref, wg_ref, wu_ref, wd_ref, acc_ref)

        @pl.when(last)
        def _():
            o_ref[...] = acc_ref[...]

    @pl.when(jnp.logical_and(n_valid == 0, last))
    def _():
        o_ref[...] = jnp.zeros(o_ref.shape, o_ref.dtype)


def _moe_ffn(xs, tile_expert, n_valid, wg, wu, wd, layer, tm):
    P = xs.shape[0]
    tf = D_FF // 2 if D_FF % 512 == 0 else _tile(D_FF, 512)
    row = lambda i, j, te, nu: (i, 0)
    up = lambda i, j, te, nu: (layer, te[i], 0, j)
    return pl.pallas_call(
        _moe_ffn_kernel,
        grid_spec=pltpu.PrefetchScalarGridSpec(
            num_scalar_prefetch=2,
            grid=(P // tm, D_FF // tf),
            in_specs=[
                pl.BlockSpec((tm, D_MODEL // 2), row),
                pl.BlockSpec((None, None, D_MODEL, tf), up),
                pl.BlockSpec((None, None, D_MODEL, tf), up),
                pl.BlockSpec((None, None, tf, D_MODEL), lambda i, j, te, nu: (layer, te[i], j, 0)),
            ],
            out_specs=pl.BlockSpec((tm, D_MODEL), row),
            scratch_shapes=[pltpu.VMEM((tm, D_MODEL), F32), pltpu.VMEM((tm, D_MODEL), BF16)],
        ),
        out_shape=jax.ShapeDtypeStruct((P, D_MODEL), F32),
        compiler_params=_params(("arbitrary", "arbitrary")),
        name="moe_ffn",
    )(tile_expert, n_valid, xs, wg, wu, wd)


def _sc_gather_rows(src, idx):
    n = idx.shape[0]
    info = plsc.get_sparse_core_info()
    n_workers = info.num_cores * info.num_subcores
    per_worker = n // n_workers
    mesh = plsc.VectorSubcoreMesh(core_axis_name="c", subcore_axis_name="s")

    @functools.partial(
        pl.kernel, mesh=mesh,
        out_type=jax.ShapeDtypeStruct((n, D_MODEL), src.dtype),
        scratch_types=[pltpu.VMEM((SC_WINDOW,), jnp.int32),
                       pltpu.VMEM((SC_WINDOW, D_MODEL), src.dtype),
                       pltpu.SemaphoreType.DMA])
    def gather(src_hbm, idx_hbm, out_hbm, idx_v, rows_v, sem):
        wid = lax.axis_index("s") * info.num_cores + lax.axis_index("c")
        base = wid * per_worker

        @pl.loop(0, per_worker, step=SC_WINDOW)
        def _(off):
            pltpu.sync_copy(idx_hbm.at[pl.ds(base + off, SC_WINDOW)], idx_v)
            pltpu.async_copy(src_hbm.at[idx_v], rows_v, sem).wait()
            pltpu.sync_copy(rows_v, out_hbm.at[pl.ds(base + off, SC_WINDOW)])

    return gather(src, idx)


def _ple_kernel(*refs, moe, last):
    refs = list(refs)
    x_ref = refs.pop(0)
    if moe:
        y0_ref, y1_ref, route_ref = refs.pop(0), refs.pop(0), refs.pop(0)
    p_ref, wg_ref, b_ref, pw_ref, pn_ref = refs[:5]
    refs = refs[5:]
    if last:
        fn_ref = refs.pop(0)
    o_ref = refs.pop(0)

    e = jnp.dot(p_ref[...].astype(BF16), pw_ref[...], preferred_element_type=F32)
    e = _rms(e) * pn_ref[...]
    x = x_ref[...]
    if moe:
        rt = route_ref[...]
        x = x + (rt[:, 2:3] * y0_ref[...] + rt[:, 3:4] * y1_ref[...])
    gate = jax.nn.sigmoid(jnp.dot(_rms(x).astype(BF16), wg_ref[...], preferred_element_type=F32)
                          + b_ref[...])
    x = x + gate * e
    if last:
        x = _rms(x) * fn_ref[...]
    o_ref[...] = x


def _ple(x, dst, ys, route, p, layer, wg, b, pw, pn, final_norm):
    T = x.shape[0]
    tm = _tile(T, 512)
    moe = ys is not None
    last = final_norm is not None
    full = lambda i: (0, 0)
    row = lambda i: (i, 0)
    in_specs = [pl.BlockSpec((tm, D_MODEL), row)]
    args = [x]
    scratch = []
    if moe:
        in_specs += [pl.BlockSpec((tm, D_MODEL), row), pl.BlockSpec((tm, D_MODEL), row),
                     pl.BlockSpec((tm, LANES), row)]
        args += [_sc_gather_rows(ys, dst[:, 0]), _sc_gather_rows(ys, dst[:, 1]), route]
    in_specs += [
        pl.BlockSpec((None, tm, PLE_DIM), lambda i: (layer, i, 0)),
        pl.BlockSpec(wg.shape, full),
        pl.BlockSpec((1, D_MODEL), full),
        pl.BlockSpec(pw.shape, full),
        pl.BlockSpec((1, D_MODEL), full),
    ]
    args += [p, wg, b, pw, pn]
    if last:
        in_specs.append(pl.BlockSpec((1, D_MODEL), full))
        args.append(final_norm)
    return pl.pallas_call(
        functools.partial(_ple_kernel, moe=moe, last=last),
        grid=(T // tm,),
        in_specs=in_specs,
        out_specs=pl.BlockSpec((tm, D_MODEL), row),
        out_shape=jax.ShapeDtypeStruct((T, D_MODEL), F32),
        scratch_shapes=scratch,
        compiler_params=_params(("parallel",)),
        name="ple" + ("_moe" if moe else "") + ("_last" if last else ""),
    )(*args)


def _rope_tables(seq):
    half = ROT_DIM // 2
    pos = jnp.arange(seq, dtype=F32)
    inv = ROPE_THETA ** (-jnp.arange(half, dtype=F32) * 2.0 / ROT_DIM)
    ang = pos[:, None] * inv[None, :]
    cos, sin = jnp.cos(ang), jnp.sin(ang)
    ones = jnp.ones((seq, HEAD_DIM - ROT_DIM), F32)
    zeros = jnp.zeros((seq, HEAD_DIM - ROT_DIM), F32)
    zh = jnp.zeros((seq, half), F32)
    c = jnp.concatenate([cos, cos, ones], axis=1)
    sa = jnp.concatenate([zh, sin, zeros], axis=1)
    sb = jnp.concatenate([-sin, zh, zeros], axis=1)
    rep = lambda t: jnp.concatenate([t, t], axis=1)
    return rep(c), rep(sa), rep(sb)


def _group_layout(route, counts, tm):
    T = route.shape[0]
    counts = counts[0, :N_EXPERTS].astype(jnp.int32)
    n_tile_e = (counts + tm - 1) // tm
    tile_end = jnp.cumsum(n_tile_e)
    start = (tile_end - n_tile_e) * tm
    expert = route[:, 0:TOP_K].astype(jnp.int32)
    rank = route[:, 4:4 + TOP_K].astype(jnp.int32)
    onehot = expert[..., None] == jnp.arange(N_EXPERTS, dtype=jnp.int32)
    dst = jnp.sum(jnp.where(onehot, start, 0), axis=-1) + rank
    n_tiles = TOP_K * T // tm + N_EXPERTS
    tile = jnp.arange(n_tiles, dtype=jnp.int32)
    tile_expert = jnp.minimum(
        jnp.sum((tile[:, None] >= tile_end[None, :]).astype(jnp.int32), axis=1), N_EXPERTS - 1)
    own = tile_expert[:, None] == jnp.arange(N_EXPERTS, dtype=jnp.int32)[None, :]
    left = jnp.sum(jnp.where(own, counts - (tile[:, None] - (tile_end - n_tile_e)) * tm, 0), axis=1)
    n_valid = jnp.where(tile < tile_end[-1], jnp.clip(left, 0, tm), 0).astype(jnp.int32)
    return dst, tile_expert, n_valid, n_tiles * tm


def _trunk(x, p, W):
    B, S, _ = x.shape
    T = B * S
    depth = p.shape[0]
    x = x.reshape(T, D_MODEL)
    p = p.reshape(depth, T, PLE_DIM)
    cos, sa, sb = _rope_tables(S)
    for i in range(depth):
        lam_init = 0.8 - 0.6 * math.exp(-0.3 * i)
        q, k, v, gm = _inproj(x, W["norm_mix"][i], W["w_in"][i], cos, sa, sb, W["gmlp_ln_g"][i],
                              W["gmlp_ln_b"][i], W["gmlp_ws"][i], W["gmlp_bs"][i], S)
        att = _attention(q, k, v, W["lam"][i], W["subln"][i], B, S, lam_init)
        moe = i % 2 == 1
        j = i // 2
        fin = W["final_norm"] if i == depth - 1 else None
        ple_w = (W["ple_gate_w"][i], W["ple_gate_b"][i], W["ple_w"][i], W["ple_norm"][i], fin)
        if not moe:
            x1, h2 = _outproj(att, gm, x, W["w_out"][i], W["norm_ffn"][i], None)
            x2 = _dense_ffn(h2, x1, W["dense_w_gate"], W["dense_w_up"], W["dense_w_down"], j)
            x = _ple(x2, None, None, None, p, i, *ple_w)
        else:
            x1, hp, route, counts = _outproj(att, gm, x, W["w_out"][i], W["norm_ffn"][i],
                                             W["router"][j])
            tm = _tile(T, MOE_TM)
            dst, tile_expert, n_valid, n_rows = _group_layout(route, counts, tm)
            xs = _sc_scatter_rows(hp, dst, n_rows)
            ys = _moe_ffn(xs, tile_expert, n_valid, W["moe_w_gate"], W["moe_w_up"],
                          W["moe_w_down"], j, tm)
            x = _ple(x1, dst, ys, route, p, i, *ple_w)
    return x.reshape(B, S, D_MODEL)


def kernel(x_prompt, x_sample, p_prompt, p_sample, norm_mix, w_in, lambda_q1, lambda_k1, lambda_q2, lambda_k2, subln, gmlp_ln_g, gmlp_ln_b, gmlp_ws, gmlp_bs, w_out, norm_ffn, dense_w_gate, dense_w_up, dense_w_down, router, moe_w_gate, moe_w_up, moe_w_down, ple_w, ple_norm, ple_gate_w, ple_gate_b, final_norm):
    depth = w_in.shape[0]
    row = lambda t: t[:, None, :]
    lam = jnp.stack([lambda_q1, lambda_k1, lambda_q2, lambda_k2], axis=1)
    lam = jnp.pad(lam, ((0, 0), (0, 4), (0, LANES - HEAD_DIM)))
    r_hi = router.astype(BF16)
    r_lo = (router - r_hi.astype(F32)).astype(BF16)
    W = dict(
        norm_mix=row(norm_mix), w_in=w_in.astype(BF16), lam=lam, subln=row(subln),
        gmlp_ln_g=row(gmlp_ln_g), gmlp_ln_b=row(gmlp_ln_b), gmlp_ws=gmlp_ws.astype(BF16),
        gmlp_bs=jnp.broadcast_to(gmlp_bs[..., None], gmlp_bs.shape + (CHUNK,)),
        w_out=w_out.astype(BF16), norm_ffn=row(norm_ffn),
        dense_w_gate=dense_w_gate.astype(BF16), dense_w_up=dense_w_up.astype(BF16),
        dense_w_down=dense_w_down.astype(BF16),
        router=jnp.pad(jnp.concatenate([r_hi, r_lo], axis=-1),
                       ((0, 0), (0, 0), (0, LANES - 2 * N_EXPERTS))),
        moe_w_gate=moe_w_gate.astype(BF16), moe_w_up=moe_w_up.astype(BF16),
        moe_w_down=moe_w_down.astype(BF16),
        ple_w=ple_w.astype(BF16), ple_norm=row(ple_norm), ple_gate_w=ple_gate_w.astype(BF16),
        ple_gate_b=row(ple_gate_b), final_norm=final_norm[None, :],
    )
    return (_trunk(x_prompt, p_prompt, W), _trunk(x_sample, p_sample, W))
```

```python
import functools
import math

import jax
import jax.numpy as jnp
from jax import lax
from jax.experimental import pallas as pl
from jax.experimental.pallas import tpu as pltpu
from jax.experimental.pallas import tpu_sc as plsc

F32 = jnp.float32
BF16 = jnp.bfloat16

D_MODEL = 1024
ATTN_WIDTH = 512
GMLP_WIDTH = 512
HEAD_DIM = 64
HEAD_WIDTH = 2 * HEAD_DIM
N_HEADS = ATTN_WIDTH // HEAD_WIDTH
ROT_DIM = HEAD_DIM // 4
ROPE_THETA = 500000.0
CHUNK = 128
N_GROUPS = 4
D_FF = 3584
N_EXPERTS = 8
TOP_K = 2
PLE_DIM = 256
EPS = 1e-6
LANES = 128
Q_SCALE = (1.0 / math.sqrt(HEAD_DIM)) * math.log2(math.e)
VMEM_LIMIT = 48 * 1024 * 1024
ATTN_TQ = 1024
ATTN_TK = 1024
ATTN_MIN_STEPS = 8
ATTN_COLS = 256
FFN_TM = 512
MOE_TM = 512
FFN_ROW_SPLIT = 2
SC_WINDOW = 32
SC_SCATTER_WINDOW = 128
ROW_TM = 1024
ROW_SPLIT = 1


def _tile(n, pref):
    t = min(n, pref)
    while n % t:
        t //= 2
    return t


def _params(sem, **kw):
    return pltpu.CompilerParams(dimension_semantics=sem, vmem_limit_bytes=VMEM_LIMIT, **kw)


def _rms(x):
    return x * lax.rsqrt(jnp.mean(x * x, axis=-1, keepdims=True) + EPS)


def _inproj_kernel(x_ref, g_ref, w_ref, cos_ref, sa_ref, sb_ref, lng_ref, lnb_ref, ws_ref, bs_ref,
                   q_ref, k_ref, v_ref, gm_ref):
    tm = x_ref.shape[0]
    rb = tm // ROW_SPLIT
    half = ROT_DIM // 2
    for s in range(ROW_SPLIT):
        rows = slice(s * rb, (s + 1) * rb)
        h = (_rms(x_ref[rows, :]) * g_ref[...]).astype(BF16)

        def proj(c0, n):
            return jnp.dot(h, w_ref[:, c0:c0 + n], preferred_element_type=F32)

        cos, sa, sb = cos_ref[rows, :], sa_ref[rows, :], sb_ref[rows, :]

        def rope(t):
            outs = []
            for hh in range(N_HEADS):
                th = t[:, hh * HEAD_WIDTH:(hh + 1) * HEAD_WIDTH]
                outs.append(th * cos + pltpu.roll(th, half, 1) * sa
                            + pltpu.roll(th, HEAD_WIDTH - half, 1) * sb)
            return jnp.concatenate(outs, axis=1)

        q_ref[rows, :] = (rope(proj(0, ATTN_WIDTH)) * Q_SCALE).astype(BF16)
        k_ref[rows, :] = rope(proj(ATTN_WIDTH, ATTN_WIDTH)).astype(BF16)
        v_ref[rows, :] = proj(2 * ATTN_WIDTH, ATTN_WIDTH).astype(BF16)

        u = jax.nn.gelu(proj(3 * ATTN_WIDTH, GMLP_WIDTH))
        vf = jax.nn.gelu(proj(3 * ATTN_WIDTH + GMLP_WIDTH, GMLP_WIDTH))
        mu = jnp.mean(vf, axis=-1, keepdims=True)
        vc = vf - mu
        var = jnp.mean(vc * vc, axis=-1, keepdims=True)
        vn = (vc * lax.rsqrt(var + EPS) * lng_ref[...] + lnb_ref[...]).astype(BF16)
        for c in range(rb // CHUNK):
            r = slice(c * CHUNK, (c + 1) * CHUNK)
            ro = slice(s * rb + c * CHUNK, s * rb + (c + 1) * CHUNK)
            for g in range(N_GROUPS):
                cg = slice(g * CHUNK, (g + 1) * CHUNK)
                mixed = jnp.dot(ws_ref[g], vn[r, cg], preferred_element_type=F32) + bs_ref[g]
                gm_ref[ro, cg] = (u[r, cg] * mixed).astype(BF16)


def _inproj(x, g, w, cos, sa, sb, lng, lnb, ws, bsb, seq):
    T = x.shape[0]
    tm = _tile(seq, ROW_TM)
    ns = seq // tm
    full = lambda i: (0, 0)
    out = jax.ShapeDtypeStruct((T, ATTN_WIDTH), BF16)
    return pl.pallas_call(
        _inproj_kernel,
        grid=(T // tm,),
        in_specs=[
            pl.BlockSpec((tm, D_MODEL), lambda i: (i, 0)),
            pl.BlockSpec((1, D_MODEL), full),
            pl.BlockSpec(w.shape, full),
            pl.BlockSpec((tm, HEAD_WIDTH), lambda i: (i % ns, 0)),
            pl.BlockSpec((tm, HEAD_WIDTH), lambda i: (i % ns, 0)),
            pl.BlockSpec((tm, HEAD_WIDTH), lambda i: (i % ns, 0)),
            pl.BlockSpec((1, GMLP_WIDTH), full),
            pl.BlockSpec((1, GMLP_WIDTH), full),
            pl.BlockSpec(ws.shape, lambda i: (0, 0, 0)),
            pl.BlockSpec(bsb.shape, lambda i: (0, 0, 0)),
        ],
        out_specs=[pl.BlockSpec((tm, ATTN_WIDTH), lambda i: (i, 0))] * 4,
        out_shape=[out] * 4,
        compiler_params=_params(("parallel",)),
        name="inproj",
    )(x, g, w, cos, sa, sb, lng, lnb, ws, bsb)


def _attn_kernel(q_ref, k_ref, v_ref, lam_ref, sub_ref, o_ref,
                 qcat_ref, sa_ref, sb_ref, m_ref, l_ref, acc_ref, *, tk, lam_init):
    tq = q_ref.shape[0]
    n = k_ref.shape[0] // tk
    qt = q_ref[...].astype(F32).T
    row = lax.broadcasted_iota(jnp.int32, qt.shape, 0)
    qcat_ref[:, :tq] = jnp.where(row < HEAD_DIM, qt, 0.0).astype(BF16)
    qcat_ref[:, tq:] = jnp.where(row >= HEAD_DIM, qt, 0.0).astype(BF16)
    m_ref[...] = jnp.full(m_ref.shape, -jnp.inf, F32)
    l_ref[...] = jnp.zeros(l_ref.shape, F32)
    acc_ref[...] = jnp.zeros(acc_ref.shape, F32)

    def rows(t):
        return pl.ds(pl.multiple_of(t * tk, tk), tk)

    def scores(t, s_ref):
        s_ref[...] = jnp.dot(k_ref[rows(t), :], qcat_ref[...], preferred_element_type=F32)

    def update(t, s_ref):
        vt = v_ref[rows(t), :]
        for cb in range(2 * tq // ATTN_COLS):
            cols = slice(cb * ATTN_COLS, (cb + 1) * ATTN_COLS)
            s = s_ref[:, cols]
            m_old = m_ref[:, cols]
            m_new = jnp.maximum(m_old, jnp.max(s, axis=0, keepdims=True))
            alpha = jnp.exp2(m_old - m_new)
            p = jnp.exp2(s - m_new)
            l_ref[:, cols] = alpha * l_ref[:, cols] + jnp.sum(p, axis=0, keepdims=True)
            m_ref[:, cols] = m_new
            pv = lax.dot_general(vt, p.astype(BF16), (((0,), (0,)), ((), ())),
                                 preferred_element_type=F32)
            acc_ref[:, cols] = alpha * acc_ref[:, cols] + pv

    scores(0, sa_ref)

    def pair(u, carry):
        t = 2 * u
        scores(t + 1, sb_ref)
        update(t, sa_ref)
        scores(t + 2, sa_ref)
        update(t + 1, sb_ref)
        return carry

    lax.fori_loop(0, n // 2 - 1, pair, 0)
    scores(n - 1, sb_ref)
    update(n - 2, sa_ref)
    update(n - 1, sb_ref)

    lp = lam_ref[...]
    lam = (jnp.exp(jnp.sum(lp[0:1] * lp[1:2], axis=-1, keepdims=True))
           - jnp.exp(jnp.sum(lp[2:3] * lp[3:4], axis=-1, keepdims=True)) + lam_init)
    l = l_ref[...]
    ot = acc_ref[:, :tq] / l[:, :tq] - lam * (acc_ref[:, tq:] / l[:, tq:])
    o = ot.T
    o_ref[...] = (_rms(o) * sub_ref[...] * (1.0 - lam_init)).astype(o_ref.dtype)


def _attention(q, k, v, lam_p, sub, batch, seq, lam_init):
    T = q.shape[0]
    tq = _tile(seq, ATTN_TQ)
    tk = _tile(seq // ATTN_MIN_STEPS, ATTN_TK)
    nq = seq // tq
    kv_spec = pl.BlockSpec((seq, HEAD_WIDTH), lambda b, h, i: (b, h))
    return pl.pallas_call(
        functools.partial(_attn_kernel, tk=tk, lam_init=lam_init),
        grid=(batch, N_HEADS, nq),
        in_specs=[
            pl.BlockSpec((tq, HEAD_WIDTH), lambda b, h, i: (b * nq + i, h)),
            kv_spec,
            kv_spec,
            pl.BlockSpec(lam_p.shape, lambda b, h, i: (0, 0)),
            pl.BlockSpec((1, HEAD_WIDTH), lambda b, h, i: (0, 0)),
        ],
        out_specs=pl.BlockSpec((tq, HEAD_WIDTH), lambda b, h, i: (b * nq + i, h)),
        out_shape=jax.ShapeDtypeStruct((T, ATTN_WIDTH), BF16),
        scratch_shapes=[
            pltpu.VMEM((HEAD_WIDTH, 2 * tq), BF16),
            pltpu.VMEM((tk, 2 * tq), F32),
            pltpu.VMEM((tk, 2 * tq), F32),
            pltpu.VMEM((1, 2 * tq), F32),
            pltpu.VMEM((1, 2 * tq), F32),
            pltpu.VMEM((HEAD_WIDTH, 2 * tq), F32),
        ],
        compiler_params=_params(("parallel", "parallel", "parallel")),
        name="attn",
    )(q, k, v, lam_p, sub)


def _pack_bf16_pairs(h):
    half = D_MODEL // 2
    h16 = h.astype(BF16)
    lo = lax.bitcast_convert_type(h16[:, :half].astype(F32), jnp.uint32)
    hi = lax.bitcast_convert_type(h16[:, half:].astype(F32), jnp.uint32)
    return (lo >> 16) | (hi & jnp.uint32(0xFFFF0000))


def _unpack_bf16_pairs(xp):
    lo = lax.bitcast_convert_type(xp << 16, F32)
    hi = lax.bitcast_convert_type(xp & jnp.uint32(0xFFFF0000), F32)
    return lo.astype(BF16), hi.astype(BF16)


def _outproj_kernel(att_ref, gm_ref, x_ref, w_ref, g_ref, *rest, moe):
    if moe:
        r_ref, x1_ref, h_ref, route_ref, count_ref, cnt_ref = rest
    else:
        x1_ref, h_ref = rest
    y = (jnp.dot(att_ref[...], w_ref[:ATTN_WIDTH, :], preferred_element_type=F32)
         + jnp.dot(gm_ref[...], w_ref[ATTN_WIDTH:, :], preferred_element_type=F32))
    x1 = x_ref[...] + y
    x1_ref[...] = x1
    h = _rms(x1) * g_ref[...]
    if not moe:
        h_ref[...] = h.astype(BF16)
        return
    h_ref[...] = _pack_bf16_pairs(h)
    tm = h.shape[0]
    h_hi = h.astype(BF16)
    h_lo = (h - h_hi.astype(F32)).astype(BF16)
    part = (jnp.dot(h_hi, r_ref[...], preferred_element_type=F32)
            + jnp.dot(h_lo, r_ref[...], preferred_element_type=F32))
    logits = part + pltpu.roll(part, LANES - N_EXPERTS, 1)
    lane = lax.broadcasted_iota(jnp.int32, logits.shape, 1)
    lg = jnp.where(lane < N_EXPERTS, logits, -jnp.inf)
    v1 = jnp.max(lg, axis=-1, keepdims=True)
    i1 = jnp.min(jnp.where(lg == v1, lane, LANES), axis=-1, keepdims=True)
    lg2 = jnp.where(lane == i1, -jnp.inf, lg)
    v2 = jnp.max(lg2, axis=-1, keepdims=True)
    i2 = jnp.min(jnp.where(lg2 == v2, lane, LANES), axis=-1, keepdims=True)
    e2 = jnp.exp(v2 - v1)
    den = 1.0 + e2

    @pl.when(pl.program_id(0) == 0)
    def _():
        cnt_ref[...] = jnp.zeros(cnt_ref.shape, F32)

    sel1 = lane == i1
    sel2 = lane == i2
    onehot = jnp.where(jnp.logical_or(sel1, sel2), 1.0, 0.0)
    r_i = lax.broadcasted_iota(jnp.int32, (tm, tm), 0)
    c_i = lax.broadcasted_iota(jnp.int32, (tm, tm), 1)
    tri = jnp.where(c_i < r_i, 1.0, 0.0).astype(BF16)
    before = jnp.dot(tri, onehot.astype(BF16), preferred_element_type=F32) + cnt_ref[...]
    rank1 = jnp.sum(jnp.where(sel1, before, 0.0), axis=-1, keepdims=True)
    rank2 = jnp.sum(jnp.where(sel2, before, 0.0), axis=-1, keepdims=True)
    cnt = cnt_ref[...] + jnp.sum(onehot, axis=0, keepdims=True)
    cnt_ref[...] = cnt
    count_ref[...] = jnp.broadcast_to(cnt, count_ref.shape)
    route = jnp.zeros(logits.shape, F32)
    for col, val in enumerate((i1.astype(F32), i2.astype(F32), 1.0 / den, e2 / den, rank1, rank2)):
        route = jnp.where(lane == col, val, route)
    route_ref[...] = route


def _outproj(att, gm, x, w, g, router_pad):
    T = x.shape[0]
    tm = _tile(T, 512)
    moe = router_pad is not None
    full = lambda i: (0, 0)
    row = lambda i: (i, 0)
    in_specs = [
        pl.BlockSpec((tm, ATTN_WIDTH), row),
        pl.BlockSpec((tm, GMLP_WIDTH), row),
        pl.BlockSpec((tm, D_MODEL), row),
        pl.BlockSpec(w.shape, full),
        pl.BlockSpec((1, D_MODEL), full),
    ]
    args = [att, gm, x, w, g]
    if not moe:
        out_specs = [pl.BlockSpec((tm, D_MODEL), row), pl.BlockSpec((tm, D_MODEL), row)]
        out_shape = [jax.ShapeDtypeStruct((T, D_MODEL), F32),
                     jax.ShapeDtypeStruct((T, D_MODEL), BF16)]
        scratch = []
    else:
        in_specs.append(pl.BlockSpec(router_pad.shape, full))
        args.append(router_pad)
        out_specs = [pl.BlockSpec((tm, D_MODEL), row), pl.BlockSpec((tm, D_MODEL // 2), row),
                     pl.BlockSpec((tm, LANES), row), pl.BlockSpec((8, LANES), full)]
        out_shape = [jax.ShapeDtypeStruct((T, D_MODEL), F32),
                     jax.ShapeDtypeStruct((T, D_MODEL // 2), jnp.uint32),
                     jax.ShapeDtypeStruct((T, LANES), F32),
                     jax.ShapeDtypeStruct((8, LANES), F32)]
        scratch = [pltpu.VMEM((1, LANES), F32)]
    return pl.pallas_call(
        functools.partial(_outproj_kernel, moe=moe),
        grid=(T // tm,),
        in_specs=in_specs,
        out_specs=out_specs,
        out_shape=out_shape,
        scratch_shapes=scratch,
        compiler_params=_params(("arbitrary",) if moe else ("parallel",)),
        name="outproj_moe" if moe else "outproj",
    )(*args)


def _sc_scatter_rows(src, dst2, n_rows):
    n, width = src.shape
    info = plsc.get_sparse_core_info()
    n_workers = info.num_cores * info.num_subcores
    per_worker = n // n_workers
    win = SC_SCATTER_WINDOW
    idx = [dst2[:, k].reshape(n // win, win) for k in range(TOP_K)]
    mesh = plsc.VectorSubcoreMesh(core_axis_name="c", subcore_axis_name="s")

    @functools.partial(
        pl.kernel, mesh=mesh,
        out_type=jax.ShapeDtypeStruct((n_rows, width), src.dtype),
        scratch_types=[pltpu.VMEM((1, win), jnp.int32), pltpu.VMEM((1, win), jnp.int32),
                       pltpu.VMEM((win, width), src.dtype)])
    def scatter(src_hbm, i0_hbm, i1_hbm, out_hbm, i0_v, i1_v, rows_v):
        wid = lax.axis_index("s") * info.num_cores + lax.axis_index("c")
        base = wid * per_worker

        @pl.loop(0, per_worker, step=win)
        def _(off):
            w = (base + off) // win
            pltpu.sync_copy(i0_hbm.at[pl.ds(w, 1)], i0_v)
            pltpu.sync_copy(i1_hbm.at[pl.ds(w, 1)], i1_v)
            pltpu.sync_copy(src_hbm.at[pl.ds(base + off, win)], rows_v)
            pltpu.sync_copy(rows_v, out_hbm.at[i0_v.at[0]])
            pltpu.sync_copy(rows_v, out_hbm.at[i1_v.at[0]])

    return scatter(src, *idx)


def _swiglu_step(h_ref, wg_ref, wu_ref, wd_ref, acc_ref):
    tm = h_ref.shape[0]
    rb = tm // FFN_ROW_SPLIT

    @pl.when(pl.program_id(1) == 0)
    def _():
        acc_ref[...] = jnp.zeros(acc_ref.shape, F32)

    for s in range(FFN_ROW_SPLIT):
        rows = slice(s * rb, (s + 1) * rb)
        h = h_ref[rows, :]
        a = (jax.nn.silu(jnp.dot(h, wg_ref[...], preferred_element_type=F32))
             * jnp.dot(h, wu_ref[...], preferred_element_type=F32)).astype(BF16)
        acc_ref[rows, :] += jnp.dot(a, wd_ref[...], preferred_element_type=F32)


def _dense_ffn_kernel(h_ref, x_ref, wg_ref, wu_ref, wd_ref, o_ref, acc_ref):
    _swiglu_step(h_ref, wg_ref, wu_ref, wd_ref, acc_ref)

    @pl.when(pl.program_id(1) == pl.num_programs(1) - 1)
    def _():
        o_ref[...] = x_ref[...] + acc_ref[...]


def _dense_ffn(h, x, wg, wu, wd, layer):
    T = h.shape[0]
    tm = _tile(T, FFN_TM)
    tf = D_FF // 2 if D_FF % 512 == 0 else _tile(D_FF, 512)
    row = lambda i, j: (i, 0)
    return pl.pallas_call(
        _dense_ffn_kernel,
        grid=(T // tm, D_FF // tf),
        in_specs=[
            pl.BlockSpec((tm, D_MODEL), row),
            pl.BlockSpec((tm, D_MODEL), row),
            pl.BlockSpec((None, D_MODEL, tf), lambda i, j: (layer, 0, j)),
            pl.BlockSpec((None, D_MODEL, tf), lambda i, j: (layer, 0, j)),
            pl.BlockSpec((None, tf, D_MODEL), lambda i, j: (layer, j, 0)),
        ],
        out_specs=pl.BlockSpec((tm, D_MODEL), row),
        out_shape=jax.ShapeDtypeStruct((T, D_MODEL), F32),
        scratch_shapes=[pltpu.VMEM((tm, D_MODEL), F32)],
        compiler_params=_params(("parallel", "arbitrary")),
        name="dense_ffn",
    )(h, x, wg, wu, wd)


def _moe_ffn_kernel(te_ref, nv_ref, xp_ref, wg_ref, wu_ref, wd_ref, o_ref, acc_ref, h_ref):
    n_valid = nv_ref[pl.program_id(0)]
    last = pl.program_id(1) == pl.num_programs(1) - 1

    @pl.when(n_valid > 0)
    def _():
        @pl.when(pl.program_id(1) == 0)
        def _():
            lo, hi = _unpack_bf16_pairs(xp_ref[...])
            live = lax.broadcasted_iota(jnp.int32, lo.shape, 0) < n_valid
            h_ref[:, :D_MODEL // 2] = jnp.where(live, lo, jnp.zeros_like(lo))
            h_ref[:, D_MODEL // 2:] = jnp.where(live, hi, jnp.zeros_like(hi))

        _swiglu_step(h_ref, wg_ref, wu_ref, wd_ref, acc_ref)

        @pl.when(last)
        def _():
            o_ref[...] = acc_ref[...]

    @pl.when(jnp.logical_and(n_valid == 0, last))
    def _():
        o_ref[...] = jnp.zeros(o_ref.shape, o_ref.dtype)


def _moe_ffn(xs, tile_expert, n_valid, wg, wu, wd, layer, tm):
    P = xs.shape[0]
    tf = D_FF // 2 if D_FF % 512 == 0 else _tile(D_FF, 512)
    row = lambda i, j, te, nu: (i, 0)
    up = lambda i, j, te, nu: (layer, te[i], 0, j)
    return pl.pallas_call(
        _moe_ffn_kernel,
        grid_spec=pltpu.PrefetchScalarGridSpec(
            num_scalar_prefetch=2,
            grid=(P // tm, D_FF // tf),
            in_specs=[
                pl.BlockSpec((tm, D_MODEL // 2), row),
                pl.BlockSpec((None, None, D_MODEL, tf), up),
                pl.BlockSpec((None, None, D_MODEL, tf), up),
                pl.BlockSpec((None, None, tf, D_MODEL), lambda i, j, te, nu: (layer, te[i], j, 0)),
            ],
            out_specs=pl.BlockSpec((tm, D_MODEL), row),
            scratch_shapes=[pltpu.VMEM((tm, D_MODEL), F32), pltpu.VMEM((tm, D_MODEL), BF16)],
        ),
        out_shape=jax.ShapeDtypeStruct((P, D_MODEL), F32),
        compiler_params=_params(("arbitrary", "arbitrary")),
        name="moe_ffn",
    )(tile_expert, n_valid, xs, wg, wu, wd)


def _sc_gather_rows(src, idx):
    n = idx.shape[0]
    info = plsc.get_sparse_core_info()
    n_workers = info.num_cores * info.num_subcores
    per_worker = n // n_workers
    mesh = plsc.VectorSubcoreMesh(core_axis_name="c", subcore_axis_name="s")

    @functools.partial(
        pl.kernel, mesh=mesh,
        out_type=jax.ShapeDtypeStruct((n, D_MODEL), src.dtype),
        scratch_types=[pltpu.VMEM((SC_WINDOW,), jnp.int32),
                       pltpu.VMEM((SC_WINDOW, D_MODEL), src.dtype),
                       pltpu.SemaphoreType.DMA])
    def gather(src_hbm, idx_hbm, out_hbm, idx_v, rows_v, sem):
        wid = lax.axis_index("s") * info.num_cores + lax.axis_index("c")
        base = wid * per_worker

        @pl.loop(0, per_worker, step=SC_WINDOW)
        def _(off):
            pltpu.sync_copy(idx_hbm.at[pl.ds(base + off, SC_WINDOW)], idx_v)
            pltpu.async_copy(src_hbm.at[idx_v], rows_v, sem).wait()
            pltpu.sync_copy(rows_v, out_hbm.at[pl.ds(base + off, SC_WINDOW)])

    return gather(src, idx)


def _ple_kernel(*refs, moe, last):
    refs = list(refs)
    x_ref = refs.pop(0)
    if moe:
        y0_ref, y1_ref, route_ref = refs.pop(0), refs.pop(0), refs.pop(0)
    p_ref, wg_ref, b_ref, pw_ref, pn_ref = refs[:5]
    refs = refs[5:]
    if last:
        fn_ref = refs.pop(0)
    o_ref = refs.pop(0)

    e = jnp.dot(p_ref[...].astype(BF16), pw_ref[...], preferred_element_type=F32)
    e = _rms(e) * pn_ref[...]
    x = x_ref[...]
    if moe:
        rt = route_ref[...]
        x = x + (rt[:, 2:3] * y0_ref[...] + rt[:, 3:4] * y1_ref[...])
    gate = jax.nn.sigmoid(jnp.dot(_rms(x).astype(BF16), wg_ref[...], preferred_element_type=F32)
                          + b_ref[...])
    x = x + gate * e
    if last:
        x = _rms(x) * fn_ref[...]
    o_ref[...] = x


def _ple(x, dst, ys, route, p, layer, wg, b, pw, pn, final_norm):
    T = x.shape[0]
    tm = _tile(T, 512)
    moe = ys is not None
    last = final_norm is not None
    full = lambda i: (0, 0)
    row = lambda i: (i, 0)
    in_specs = [pl.BlockSpec((tm, D_MODEL), row)]
    args = [x]
    scratch = []
    if moe:
        in_specs += [pl.BlockSpec((tm, D_MODEL), row), pl.BlockSpec((tm, D_MODEL), row),
                     pl.BlockSpec((tm, LANES), row)]
        args += [_sc_gather_rows(ys, dst[:, 0]), _sc_gather_rows(ys, dst[:, 1]), route]
    in_specs += [
        pl.BlockSpec((None, tm, PLE_DIM), lambda i: (layer, i, 0)),
        pl.BlockSpec(wg.shape, full),
        pl.BlockSpec((1, D_MODEL), full),
        pl.BlockSpec(pw.shape, full),
        pl.BlockSpec((1, D_MODEL), full),
    ]
    args += [p, wg, b, pw, pn]
    if last:
        in_specs.append(pl.BlockSpec((1, D_MODEL), full))
        args.append(final_norm)
    return pl.pallas_call(
        functools.partial(_ple_kernel, moe=moe, last=last),
        grid=(T // tm,),
        in_specs=in_specs,
        out_specs=pl.BlockSpec((tm, D_MODEL), row),
        out_shape=jax.ShapeDtypeStruct((T, D_MODEL), F32),
        scratch_shapes=scratch,
        compiler_params=_params(("parallel",)),
        name="ple" + ("_moe" if moe else "") + ("_last" if last else ""),
    )(*args)


def _rope_tables(seq):
    half = ROT_DIM // 2
    pos = jnp.arange(seq, dtype=F32)
    inv = ROPE_THETA ** (-jnp.arange(half, dtype=F32) * 2.0 / ROT_DIM)
    ang = pos[:, None] * inv[None, :]
    cos, sin = jnp.cos(ang), jnp.sin(ang)
    ones = jnp.ones((seq, HEAD_DIM - ROT_DIM), F32)
    zeros = jnp.zeros((seq, HEAD_DIM - ROT_DIM), F32)
    zh = jnp.zeros((seq, half), F32)
    c = jnp.concatenate([cos, cos, ones], axis=1)
    sa = jnp.concatenate([zh, sin, zeros], axis=1)
    sb = jnp.concatenate([-sin, zh, zeros], axis=1)
    rep = lambda t: jnp.concatenate([t, t], axis=1)
    return rep(c), rep(sa), rep(sb)


def _group_layout(route, counts, tm):
    T = route.shape[0]
    counts = counts[0, :N_EXPERTS].astype(jnp.int32)
    n_tile_e = (counts + tm - 1) // tm
    tile_end = jnp.cumsum(n_tile_e)
    start = (tile_end - n_tile_e) * tm
    expert = route[:, 0:TOP_K].astype(jnp.int32)
    rank = route[:, 4:4 + TOP_K].astype(jnp.int32)
    onehot = expert[..., None] == jnp.arange(N_EXPERTS, dtype=jnp.int32)
    dst = jnp.sum(jnp.where(onehot, start, 0), axis=-1) + rank
    n_tiles = TOP_K * T // tm + N_EXPERTS
    tile = jnp.arange(n_tiles, dtype=jnp.int32)
    tile_expert = jnp.minimum(
        jnp.sum((tile[:, None] >= tile_end[None, :]).astype(jnp.int32), axis=1), N_EXPERTS - 1)
    own = tile_expert[:, None] == jnp.arange(N_EXPERTS, dtype=jnp.int32)[None, :]
    left = jnp.sum(jnp.where(own, counts - (tile[:, None] - (tile_end - n_tile_e)) * tm, 0), axis=1)
    n_valid = jnp.where(tile < tile_end[-1], jnp.clip(left, 0, tm), 0).astype(jnp.int32)
    return dst, tile_expert, n_valid, n_tiles * tm


def _trunk(x, p, W):
    B, S, _ = x.shape
    T = B * S
    depth = p.shape[0]
    x = x.reshape(T, D_MODEL)
    p = p.reshape(depth, T, PLE_DIM)
    cos, sa, sb = _rope_tables(S)
    for i in range(depth):
        lam_init = 0.8 - 0.6 * math.exp(-0.3 * i)
        q, k, v, gm = _inproj(x, W["norm_mix"][i], W["w_in"][i], cos, sa, sb, W["gmlp_ln_g"][i],
                              W["gmlp_ln_b"][i], W["gmlp_ws"][i], W["gmlp_bs"][i], S)
        att = _attention(q, k, v, W["lam"][i], W["subln"][i], B, S, lam_init)
        moe = i % 2 == 1
        j = i // 2
        fin = W["final_norm"] if i == depth - 1 else None
        ple_w = (W["ple_gate_w"][i], W["ple_gate_b"][i], W["ple_w"][i], W["ple_norm"][i], fin)
        if not moe:
            x1, h2 = _outproj(att, gm, x, W["w_out"][i], W["norm_ffn"][i], None)
            x2 = _dense_ffn(h2, x1, W["dense_w_gate"], W["dense_w_up"], W["dense_w_down"], j)
            x = _ple(x2, None, None, None, p, i, *ple_w)
        else:
            x1, hp, route, counts = _outproj(att, gm, x, W["w_out"][i], W["norm_ffn"][i],
                                             W["router"][j])
            tm = _tile(T, MOE_TM)
            dst, tile_expert, n_valid, n_rows = _group_layout(route, counts, tm)
            xs = _sc_scatter_rows(hp, dst, n_rows)
            ys = _moe_ffn(xs, tile_expert, n_valid, W["moe_w_gate"], W["moe_w_up"],
                          W["moe_w_down"], j, tm)
            x = _ple(x1, dst, ys, route, p, i, *ple_w)
    return x.reshape(B, S, D_MODEL)


def kernel(x_prompt, x_sample, p_prompt, p_sample, norm_mix, w_in, lambda_q1, lambda_k1, lambda_q2, lambda_k2, subln, gmlp_ln_g, gmlp_ln_b, gmlp_ws, gmlp_bs, w_out, norm_ffn, dense_w_gate, dense_w_up, dense_w_down, router, moe_w_gate, moe_w_up, moe_w_down, ple_w, ple_norm, ple_gate_w, ple_gate_b, final_norm):
    depth = w_in.shape[0]
    row = lambda t: t[:, None, :]
    lam = jnp.stack([lambda_q1, lambda_k1, lambda_q2, lambda_k2], axis=1)
    lam = jnp.pad(lam, ((0, 0), (0, 4), (0, LANES - HEAD_DIM)))
    r_hi = router.astype(BF16)
    r_lo = (router - r_hi.astype(F32)).astype(BF16)
    W = dict(
        norm_mix=row(norm_mix), w_in=w_in.astype(BF16), lam=lam, subln=row(subln),
        gmlp_ln_g=row(gmlp_ln_g), gmlp_ln_b=row(gmlp_ln_b), gmlp_ws=gmlp_ws.astype(BF16),
        gmlp_bs=jnp.broadcast_to(gmlp_bs[..., None], gmlp_bs.shape + (CHUNK,)),
        w_out=w_out.astype(BF16), norm_ffn=row(norm_ffn),
        dense_w_gate=dense_w_gate.astype(BF16), dense_w_up=dense_w_up.astype(BF16),
        dense_w_down=dense_w_down.astype(BF16),
        router=jnp.pad(jnp.concatenate([r_hi, r_lo], axis=-1),
                       ((0, 0), (0, 0), (0, LANES - 2 * N_EXPERTS))),
        moe_w_gate=moe_w_gate.astype(BF16), moe_w_up=moe_w_up.astype(BF16),
        moe_w_down=moe_w_down.astype(BF16),
        ple_w=ple_w.astype(BF16), ple_norm=row(ple_norm), ple_gate_w=ple_gate_w.astype(BF16),
        ple_gate_b=row(ple_gate_b), final_norm=final_norm[None, :],
    )
    return (_trunk(x_prompt, p_prompt, W), _trunk(x_sample, p_sample, W))
```

```python
import functools
import math

import jax
import jax.numpy as jnp
from jax import lax
from jax.experimental import pallas as pl
from jax.experimental.pallas import tpu as pltpu
from jax.experimental.pallas import tpu_sc as plsc

F32 = jnp.float32
BF16 = jnp.bfloat16

D_MODEL = 1024
ATTN_WIDTH = 512
GMLP_WIDTH = 512
HEAD_DIM = 64
HEAD_WIDTH = 2 * HEAD_DIM
N_HEADS = ATTN_WIDTH // HEAD_WIDTH
ROT_DIM = HEAD_DIM // 4
ROPE_THETA = 500000.0
CHUNK = 128
N_GROUPS = 4
D_FF = 3584
N_EXPERTS = 8
TOP_K = 2
PLE_DIM = 256
EPS = 1e-6
LANES = 128
Q_SCALE = (1.0 / math.sqrt(HEAD_DIM)) * math.log2(math.e)
VMEM_LIMIT = 48 * 1024 * 1024
ATTN_TQ = 1024
ATTN_TK = 1024
ATTN_MIN_STEPS = 8
ATTN_COLS = 512
FFN_TM = 512
MOE_TM = 512
FFN_ROW_SPLIT = 2
SC_WINDOW = 32
SC_SCATTER_WINDOW = 128
ROW_TM = 1024
ROW_SPLIT = 1


def _tile(n, pref):
    t = min(n, pref)
    while n % t:
        t //= 2
    return t


def _params(sem, **kw):
    return pltpu.CompilerParams(dimension_semantics=sem, vmem_limit_bytes=VMEM_LIMIT, **kw)


def _rms(x):
    return x * lax.rsqrt(jnp.mean(x * x, axis=-1, keepdims=True) + EPS)


def _inproj_kernel(x_ref, g_ref, w_ref, cos_ref, sa_ref, sb_ref, lng_ref, lnb_ref, ws_ref, bs_ref,
                   q_ref, k_ref, v_ref, gm_ref):
    tm = x_ref.shape[0]
    rb = tm // ROW_SPLIT
    half = ROT_DIM // 2
    for s in range(ROW_SPLIT):
        rows = slice(s * rb, (s + 1) * rb)
        h = (_rms(x_ref[rows, :]) * g_ref[...]).astype(BF16)

        def proj(c0, n):
            return jnp.dot(h, w_ref[:, c0:c0 + n], preferred_element_type=F32)

        cos, sa, sb = cos_ref[rows, :], sa_ref[rows, :], sb_ref[rows, :]

        def rope(t):
            outs = []
            for hh in range(N_HEADS):
                th = t[:, hh * HEAD_WIDTH:(hh + 1) * HEAD_WIDTH]
                outs.append(th * cos + pltpu.roll(th, half, 1) * sa
                            + pltpu.roll(th, HEAD_WIDTH - half, 1) * sb)
            return jnp.concatenate(outs, axis=1)

        q_ref[rows, :] = (rope(proj(0, ATTN_WIDTH)) * Q_SCALE).astype(BF16)
        k_ref[rows, :] = rope(proj(ATTN_WIDTH, ATTN_WIDTH)).astype(BF16)
        v_ref[rows, :] = proj(2 * ATTN_WIDTH, ATTN_WIDTH).astype(BF16)

        u = jax.nn.gelu(proj(3 * ATTN_WIDTH, GMLP_WIDTH))
        vf = jax.nn.gelu(proj(3 * ATTN_WIDTH + GMLP_WIDTH, GMLP_WIDTH))
        mu = jnp.mean(vf, axis=-1, keepdims=True)
        vc = vf - mu
        var = jnp.mean(vc * vc, axis=-1, keepdims=True)
        vn = (vc * lax.rsqrt(var + EPS) * lng_ref[...] + lnb_ref[...]).astype(BF16)
        for c in range(rb // CHUNK):
            r = slice(c * CHUNK, (c + 1) * CHUNK)
            ro = slice(s * rb + c * CHUNK, s * rb + (c + 1) * CHUNK)
            for g in range(N_GROUPS):
                cg = slice(g * CHUNK, (g + 1) * CHUNK)
                mixed = jnp.dot(ws_ref[g], vn[r, cg], preferred_element_type=F32) + bs_ref[g]
                gm_ref[ro, cg] = (u[r, cg] * mixed).astype(BF16)


def _inproj(x, g, w, cos, sa, sb, lng, lnb, ws, bsb, seq):
    T = x.shape[0]
    tm = _tile(seq, ROW_TM)
    ns = seq // tm
    full = lambda i: (0, 0)
    out = jax.ShapeDtypeStruct((T, ATTN_WIDTH), BF16)
    return pl.pallas_call(
        _inproj_kernel,
        grid=(T // tm,),
        in_specs=[
            pl.BlockSpec((tm, D_MODEL), lambda i: (i, 0)),
            pl.BlockSpec((1, D_MODEL), full),
            pl.BlockSpec(w.shape, full),
            pl.BlockSpec((tm, HEAD_WIDTH), lambda i: (i % ns, 0)),
            pl.BlockSpec((tm, HEAD_WIDTH), lambda i: (i % ns, 0)),
            pl.BlockSpec((tm, HEAD_WIDTH), lambda i: (i % ns, 0)),
            pl.BlockSpec((1, GMLP_WIDTH), full),
            pl.BlockSpec((1, GMLP_WIDTH), full),
            pl.BlockSpec(ws.shape, lambda i: (0, 0, 0)),
            pl.BlockSpec(bsb.shape, lambda i: (0, 0, 0)),
        ],
        out_specs=[pl.BlockSpec((tm, ATTN_WIDTH), lambda i: (i, 0))] * 4,
        out_shape=[out] * 4,
        compiler_params=_params(("parallel",)),
        name="inproj",
    )(x, g, w, cos, sa, sb, lng, lnb, ws, bsb)


def _attn_kernel(q_ref, k_ref, v_ref, lam_ref, sub_ref, o_ref,
                 qcat_ref, sa_ref, sb_ref, m_ref, l_ref, acc_ref, *, tk, lam_init):
    tq = q_ref.shape[0]
    n = k_ref.shape[0] // tk
    qt = q_ref[...].astype(F32).T
    row = lax.broadcasted_iota(jnp.int32, qt.shape, 0)
    qcat_ref[:, :tq] = jnp.where(row < HEAD_DIM, qt, 0.0).astype(BF16)
    qcat_ref[:, tq:] = jnp.where(row >= HEAD_DIM, qt, 0.0).astype(BF16)
    m_ref[...] = jnp.full(m_ref.shape, -jnp.inf, F32)
    l_ref[...] = jnp.zeros(l_ref.shape, F32)
    acc_ref[...] = jnp.zeros(acc_ref.shape, F32)

    def rows(t):
        return pl.ds(pl.multiple_of(t * tk, tk), tk)

    def scores(t, s_ref):
        s_ref[...] = jnp.dot(k_ref[rows(t), :], qcat_ref[...], preferred_element_type=F32)

    def update(t, s_ref):
        vt = v_ref[rows(t), :]
        for cb in range(2 * tq // ATTN_COLS):
            cols = slice(cb * ATTN_COLS, (cb + 1) * ATTN_COLS)
            s = s_ref[:, cols]
            m_old = m_ref[:, cols]
            m_new = jnp.maximum(m_old, jnp.max(s, axis=0, keepdims=True))
            alpha = jnp.exp2(m_old - m_new)
            p = jnp.exp2(s - m_new)
            l_ref[:, cols] = alpha * l_ref[:, cols] + jnp.sum(p, axis=0, keepdims=True)
            m_ref[:, cols] = m_new
            pv = lax.dot_general(vt, p.astype(BF16), (((0,), (0,)), ((), ())),
                                 preferred_element_type=F32)
            acc_ref[:, cols] = alpha * acc_ref[:, cols] + pv

    scores(0, sa_ref)

    def pair(u, carry):
        t = 2 * u
        scores(t + 1, sb_ref)
        update(t, sa_ref)
        scores(t + 2, sa_ref)
        update(t + 1, sb_ref)
        return carry

    lax.fori_loop(0, n // 2 - 1, pair, 0)
    scores(n - 1, sb_ref)
    update(n - 2, sa_ref)
    update(n - 1, sb_ref)

    lp = lam_ref[...]
    lam = (jnp.exp(jnp.sum(lp[0:1] * lp[1:2], axis=-1, keepdims=True))
           - jnp.exp(jnp.sum(lp[2:3] * lp[3:4], axis=-1, keepdims=True)) + lam_init)
    l = l_ref[...]
    ot = acc_ref[:, :tq] / l[:, :tq] - lam * (acc_ref[:, tq:] / l[:, tq:])
    o = ot.T
    o_ref[...] = (_rms(o) * sub_ref[...] * (1.0 - lam_init)).astype(o_ref.dtype)


def _attention(q, k, v, lam_p, sub, batch, seq, lam_init):
    T = q.shape[0]
    tq = _tile(seq, ATTN_TQ)
    tk = _tile(seq // ATTN_MIN_STEPS, ATTN_TK)
    nq = seq // tq
    kv_spec = pl.BlockSpec((seq, HEAD_WIDTH), lambda b, h, i: (b, h))
    return pl.pallas_call(
        functools.partial(_attn_kernel, tk=tk, lam_init=lam_init),
        grid=(batch, N_HEADS, nq),
        in_specs=[
            pl.BlockSpec((tq, HEAD_WIDTH), lambda b, h, i: (b * nq + i, h)),
            kv_spec,
            kv_spec,
            pl.BlockSpec(lam_p.shape, lambda b, h, i: (0, 0)),
            pl.BlockSpec((1, HEAD_WIDTH), lambda b, h, i: (0, 0)),
        ],
        out_specs=pl.BlockSpec((tq, HEAD_WIDTH), lambda b, h, i: (b * nq + i, h)),
        out_shape=jax.ShapeDtypeStruct((T, ATTN_WIDTH), BF16),
        scratch_shapes=[
            pltpu.VMEM((HEAD_WIDTH, 2 * tq), BF16),
            pltpu.VMEM((tk, 2 * tq), F32),
            pltpu.VMEM((tk, 2 * tq), F32),
            pltpu.VMEM((1, 2 * tq), F32),
            pltpu.VMEM((1, 2 * tq), F32),
            pltpu.VMEM((HEAD_WIDTH, 2 * tq), F32),
        ],
        compiler_params=_params(("parallel", "parallel", "parallel")),
        name="attn",
    )(q, k, v, lam_p, sub)


def _pack_bf16_pairs(h):
    half = D_MODEL // 2
    h16 = h.astype(BF16)
    lo = lax.bitcast_convert_type(h16[:, :half].astype(F32), jnp.uint32)
    hi = lax.bitcast_convert_type(h16[:, half:].astype(F32), jnp.uint32)
    return (lo >> 16) | (hi & jnp.uint32(0xFFFF0000))


def _unpack_bf16_pairs(xp):
    lo = lax.bitcast_convert_type(xp << 16, F32)
    hi = lax.bitcast_convert_type(xp & jnp.uint32(0xFFFF0000), F32)
    return lo.astype(BF16), hi.astype(BF16)


def _outproj_kernel(att_ref, gm_ref, x_ref, w_ref, g_ref, *rest, moe):
    if moe:
        r_ref, x1_ref, h_ref, route_ref, count_ref, cnt_ref = rest
    else:
        x1_ref, h_ref = rest
    y = (jnp.dot(att_ref[...], w_ref[:ATTN_WIDTH, :], preferred_element_type=F32)
         + jnp.dot(gm_ref[...], w_ref[ATTN_WIDTH:, :], preferred_element_type=F32))
    x1 = x_ref[...] + y
    x1_ref[...] = x1
    h = _rms(x1) * g_ref[...]
    if not moe:
        h_ref[...] = h.astype(BF16)
        return
    h_ref[...] = _pack_bf16_pairs(h)
    tm = h.shape[0]
    h_hi = h.astype(BF16)
    h_lo = (h - h_hi.astype(F32)).astype(BF16)
    part = (jnp.dot(h_hi, r_ref[...], preferred_element_type=F32)
            + jnp.dot(h_lo, r_ref[...], preferred_element_type=F32))
    logits = part + pltpu.roll(part, LANES - N_EXPERTS, 1)
    lane = lax.broadcasted_iota(jnp.int32, logits.shape, 1)
    lg = jnp.where(lane < N_EXPERTS, logits, -jnp.inf)
    v1 = jnp.max(lg, axis=-1, keepdims=True)
    i1 = jnp.min(jnp.where(lg == v1, lane, LANES), axis=-1, keepdims=True)
    lg2 = jnp.where(lane == i1, -jnp.inf, lg)
    v2 = jnp.max(lg2, axis=-1, keepdims=True)
    i2 = jnp.min(jnp.where(lg2 == v2, lane, LANES), axis=-1, keepdims=True)
    e2 = jnp.exp(v2 - v1)
    den = 1.0 + e2

    @pl.when(pl.program_id(0) == 0)
    def _():
        cnt_ref[...] = jnp.zeros(cnt_ref.shape, F32)

    sel1 = lane == i1
    sel2 = lane == i2
    onehot = jnp.where(jnp.logical_or(sel1, sel2), 1.0, 0.0)
    r_i = lax.broadcasted_iota(jnp.int32, (tm, tm), 0)
    c_i = lax.broadcasted_iota(jnp.int32, (tm, tm), 1)
    tri = jnp.where(c_i < r_i, 1.0, 0.0).astype(BF16)
    before = jnp.dot(tri, onehot.astype(BF16), preferred_element_type=F32) + cnt_ref[...]
    rank1 = jnp.sum(jnp.where(sel1, before, 0.0), axis=-1, keepdims=True)
    rank2 = jnp.sum(jnp.where(sel2, before, 0.0), axis=-1, keepdims=True)
    cnt = cnt_ref[...] + jnp.sum(onehot, axis=0, keepdims=True)
    cnt_ref[...] = cnt
    count_ref[...] = jnp.broadcast_to(cnt, count_ref.shape)
    route = jnp.zeros(logits.shape, F32)
    for col, val in enumerate((i1.astype(F32), i2.astype(F32), 1.0 / den, e2 / den, rank1, rank2)):
        route = jnp.where(lane == col, val, route)
    route_ref[...] = route


def _outproj(att, gm, x, w, g, router_pad):
    T = x.shape[0]
    tm = _tile(T, 512)
    moe = router_pad is not None
    full = lambda i: (0, 0)
    row = lambda i: (i, 0)
    in_specs = [
        pl.BlockSpec((tm, ATTN_WIDTH), row),
        pl.BlockSpec((tm, GMLP_WIDTH), row),
        pl.BlockSpec((tm, D_MODEL), row),
        pl.BlockSpec(w.shape, full),
        pl.BlockSpec((1, D_MODEL), full),
    ]
    args = [att, gm, x, w, g]
    if not moe:
        out_specs = [pl.BlockSpec((tm, D_MODEL), row), pl.BlockSpec((tm, D_MODEL), row)]
        out_shape = [jax.ShapeDtypeStruct((T, D_MODEL), F32),
                     jax.ShapeDtypeStruct((T, D_MODEL), BF16)]
        scratch = []
    else:
        in_specs.append(pl.BlockSpec(router_pad.shape, full))
        args.append(router_pad)
        out_specs = [pl.BlockSpec((tm, D_MODEL), row), pl.BlockSpec((tm, D_MODEL // 2), row),
                     pl.BlockSpec((tm, LANES), row), pl.BlockSpec((8, LANES), full)]
        out_shape = [jax.ShapeDtypeStruct((T, D_MODEL), F32),
                     jax.ShapeDtypeStruct((T, D_MODEL // 2), jnp.uint32),
                     jax.ShapeDtypeStruct((T, LANES), F32),
                     jax.ShapeDtypeStruct((8, LANES), F32)]
        scratch = [pltpu.VMEM((1, LANES), F32)]
    return pl.pallas_call(
        functools.partial(_outproj_kernel, moe=moe),
        grid=(T // tm,),
        in_specs=in_specs,
        out_specs=out_specs,
        out_shape=out_shape,
        scratch_shapes=scratch,
        compiler_params=_params(("arbitrary",) if moe else ("parallel",)),
        name="outproj_moe" if moe else "outproj",
    )(*args)


def _sc_scatter_rows(src, dst2, n_rows):
    n, width = src.shape
    info = plsc.get_sparse_core_info()
    n_workers = info.num_cores * info.num_subcores
    per_worker = n // n_workers
    win = SC_SCATTER_WINDOW
    idx = [dst2[:, k].reshape(n // win, win) for k in range(TOP_K)]
    mesh = plsc.VectorSubcoreMesh(core_axis_name="c", subcore_axis_name="s")

    @functools.partial(
        pl.kernel, mesh=mesh,
        out_type=jax.ShapeDtypeStruct((n_rows, width), src.dtype),
        scratch_types=[pltpu.VMEM((1, win), jnp.int32), pltpu.VMEM((1, win), jnp.int32),
                       pltpu.VMEM((win, width), src.dtype)])
    def scatter(src_hbm, i0_hbm, i1_hbm, out_hbm, i0_v, i1_v, rows_v):
        wid = lax.axis_index("s") * info.num_cores + lax.axis_index("c")
        base = wid * per_worker

        @pl.loop(0, per_worker, step=win)
        def _(off):
            w = (base + off) // win
            pltpu.sync_copy(i0_hbm.at[pl.ds(w, 1)], i0_v)
            pltpu.sync_copy(i1_hbm.at[pl.ds(w, 1)], i1_v)
            pltpu.sync_copy(src_hbm.at[pl.ds(base + off, win)], rows_v)
            pltpu.sync_copy(rows_v, out_hbm.at[i0_v.at[0]])
            pltpu.sync_copy(rows_v, out_hbm.at[i1_v.at[0]])

    return scatter(src, *idx)


def _swiglu_step(h_ref, wg_ref, wu_ref, wd_ref, acc_ref):
    tm = h_ref.shape[0]
    rb = tm // FFN_ROW_SPLIT

    @pl.when(pl.program_id(1) == 0)
    def _():
        acc_ref[...] = jnp.zeros(acc_ref.shape, F32)

    for s in range(FFN_ROW_SPLIT):
        rows = slice(s * rb, (s + 1) * rb)
        h = h_ref[rows, :]
        a = (jax.nn.silu(jnp.dot(h, wg_ref[...], preferred_element_type=F32))
             * jnp.dot(h, wu_ref[...], preferred_element_type=F32)).astype(BF16)
        acc_ref[rows, :] += jnp.dot(a, wd_ref[...], preferred_element_type=F32)


def _dense_ffn_kernel(h_ref, x_ref, wg_ref, wu_ref, wd_ref, o_ref, acc_ref):
    _swiglu_step(h_ref, wg_ref, wu_ref, wd_ref, acc_ref)

    @pl.when(pl.program_id(1) == pl.num_programs(1) - 1)
    def _():
        o_ref[...] = x_ref[...] + acc_ref[...]


def _dense_ffn(h, x, wg, wu, wd, layer):
    T = h.shape[0]
    tm = _tile(T, FFN_TM)
    tf = D_FF // 2 if D_FF % 512 == 0 else _tile(D_FF, 512)
    row = lambda i, j: (i, 0)
    return pl.pallas_call(
        _dense_ffn_kernel,
        grid=(T // tm, D_FF // tf),
        in_specs=[
            pl.BlockSpec((tm, D_MODEL), row),
            pl.BlockSpec((tm, D_MODEL), row),
            pl.BlockSpec((None, D_MODEL, tf), lambda i, j: (layer, 0, j)),
            pl.BlockSpec((None, D_MODEL, tf), lambda i, j: (layer, 0, j)),
            pl.BlockSpec((None, tf, D_MODEL), lambda i, j: (layer, j, 0)),
        ],
        out_specs=pl.BlockSpec((tm, D_MODEL), row),
        out_shape=jax.ShapeDtypeStruct((T, D_MODEL), F32),
        scratch_shapes=[pltpu.VMEM((tm, D_MODEL), F32)],
        compiler_params=_params(("parallel", "arbitrary")),
        name="dense_ffn",
    )(h, x, wg, wu, wd)


def _moe_ffn_kernel(te_ref, nv_ref, xp_ref, wg_ref, wu_ref, wd_ref, o_ref, acc_ref, h_ref):
    n_valid = nv_ref[pl.program_id(0)]
    last = pl.program_id(1) == pl.num_programs(1) - 1

    @pl.when(n_valid > 0)
    def _():
        @pl.when(pl.program_id(1) == 0)
        def _():
            lo, hi = _unpack_bf16_pairs(xp_ref[...])
            live = lax.broadcasted_iota(jnp.int32, lo.shape, 0) < n_valid
            h_ref[:, :D_MODEL // 2] = jnp.where(live, lo, jnp.zeros_like(lo))
            h_ref[:, D_MODEL // 2:] = jnp.where(live, hi, jnp.zeros_like(hi))

        _swiglu_step(h_ref, wg_ref, wu_ref, wd_ref, acc_ref)

        @pl.when(last)
        def _():
            o_ref[...] = acc_ref[...]

    @pl.when(jnp.logical_and(n_valid == 0, last))
    def _():
        o_ref[...] = jnp.zeros(o_ref.shape, o_ref.dtype)


def _moe_ffn(xs, tile_expert, n_valid, wg, wu, wd, layer, tm):
    P = xs.shape[0]
    tf = D_FF // 2 if D_FF % 512 == 0 else _tile(D_FF, 512)
    row = lambda i, j, te, nu: (i, 0)
    up = lambda i, j, te, nu: (layer, te[i], 0, j)
    return pl.pallas_call(
        _moe_ffn_kernel,
        grid_spec=pltpu.PrefetchScalarGridSpec(
            num_scalar_prefetch=2,
            grid=(P // tm, D_FF // tf),
            in_specs=[
                pl.BlockSpec((tm, D_MODEL // 2), row),
                pl.BlockSpec((None, None, D_MODEL, tf), up),
                pl.BlockSpec((None, None, D_MODEL, tf), up),
                pl.BlockSpec((None, None, tf, D_MODEL), lambda i, j, te, nu: (layer, te[i], j, 0)),
            ],
            out_specs=pl.BlockSpec((tm, D_MODEL), row),
            scratch_shapes=[pltpu.VMEM((tm, D_MODEL), F32), pltpu.VMEM((tm, D_MODEL), BF16)],
        ),
        out_shape=jax.ShapeDtypeStruct((P, D_MODEL), F32),
        compiler_params=_params(("arbitrary", "arbitrary")),
        name="moe_ffn",
    )(tile_expert, n_valid, xs, wg, wu, wd)


def _sc_gather_rows(src, idx):
    n = idx.shape[0]
    info = plsc.get_sparse_core_info()
    n_workers = info.num_cores * info.num_subcores
    per_worker = n // n_workers
    mesh = plsc.VectorSubcoreMesh(core_axis_name="c", subcore_axis_name="s")

    @functools.partial(
        pl.kernel, mesh=mesh,
        out_type=jax.ShapeDtypeStruct((n, D_MODEL), src.dtype),
        scratch_types=[pltpu.VMEM((SC_WINDOW,), jnp.int32),
                       pltpu.VMEM((SC_WINDOW, D_MODEL), src.dtype),
                       pltpu.SemaphoreType.DMA])
    def gather(src_hbm, idx_hbm, out_hbm, idx_v, rows_v, sem):
        wid = lax.axis_index("s") * info.num_cores + lax.axis_index("c")
        base = wid * per_worker

        @pl.loop(0, per_worker, step=SC_WINDOW)
        def _(off):
            pltpu.sync_copy(idx_hbm.at[pl.ds(base + off, SC_WINDOW)], idx_v)
            pltpu.async_copy(src_hbm.at[idx_v], rows_v, sem).wait()
            pltpu.sync_copy(rows_v, out_hbm.at[pl.ds(base + off, SC_WINDOW)])

    return gather(src, idx)


def _ple_kernel(*refs, moe, last):
    refs = list(refs)
    x_ref = refs.pop(0)
    if moe:
        y0_ref, y1_ref, route_ref = refs.pop(0), refs.pop(0), refs.pop(0)
    p_ref, wg_ref, b_ref, pw_ref, pn_ref = refs[:5]
    refs = refs[5:]
    if last:
        fn_ref = refs.pop(0)
    o_ref = refs.pop(0)

    e = jnp.dot(p_ref[...].astype(BF16), pw_ref[...], preferred_element_type=F32)
    e = _rms(e) * pn_ref[...]
    x = x_ref[...]
    if moe:
        rt = route_ref[...]
        x = x + (rt[:, 2:3] * y0_ref[...] + rt[:, 3:4] * y1_ref[...])
    gate = jax.nn.sigmoid(jnp.dot(_rms(x).astype(BF16), wg_ref[...], preferred_element_type=F32)
                          + b_ref[...])
    x = x + gate * e
    if last:
        x = _rms(x) * fn_ref[...]
    o_ref[...] = x


def _ple(x, dst, ys, route, p, layer, wg, b, pw, pn, final_norm):
    T = x.shape[0]
    tm = _tile(T, 512)
    moe = ys is not None
    last = final_norm is not None
    full = lambda i: (0, 0)
    row = lambda i: (i, 0)
    in_specs = [pl.BlockSpec((tm, D_MODEL), row)]
    args = [x]
    scratch = []
    if moe:
        in_specs += [pl.BlockSpec((tm, D_MODEL), row), pl.BlockSpec((tm, D_MODEL), row),
                     pl.BlockSpec((tm, LANES), row)]
        args += [_sc_gather_rows(ys, dst[:, 0]), _sc_gather_rows(ys, dst[:, 1]), route]
    in_specs += [
        pl.BlockSpec((None, tm, PLE_DIM), lambda i: (layer, i, 0)),
        pl.BlockSpec(wg.shape, full),
        pl.BlockSpec((1, D_MODEL), full),
        pl.BlockSpec(pw.shape, full),
        pl.BlockSpec((1, D_MODEL), full),
    ]
    args += [p, wg, b, pw, pn]
    if last:
        in_specs.append(pl.BlockSpec((1, D_MODEL), full))
        args.append(final_norm)
    return pl.pallas_call(
        functools.partial(_ple_kernel, moe=moe, last=last),
        grid=(T // tm,),
        in_specs=in_specs,
        out_specs=pl.BlockSpec((tm, D_MODEL), row),
        out_shape=jax.ShapeDtypeStruct((T, D_MODEL), F32),
        scratch_shapes=scratch,
        compiler_params=_params(("parallel",)),
        name="ple" + ("_moe" if moe else "") + ("_last" if last else ""),
    )(*args)


def _rope_tables(seq):
    half = ROT_DIM // 2
    pos = jnp.arange(seq, dtype=F32)
    inv = ROPE_THETA ** (-jnp.arange(half, dtype=F32) * 2.0 / ROT_DIM)
    ang = pos[:, None] * inv[None, :]
    cos, sin = jnp.cos(ang), jnp.sin(ang)
    ones = jnp.ones((seq, HEAD_DIM - ROT_DIM), F32)
    zeros = jnp.zeros((seq, HEAD_DIM - ROT_DIM), F32)
    zh = jnp.zeros((seq, half), F32)
    c = jnp.concatenate([cos, cos, ones], axis=1)
    sa = jnp.concatenate([zh, sin, zeros], axis=1)
    sb = jnp.concatenate([-sin, zh, zeros], axis=1)
    rep = lambda t: jnp.concatenate([t, t], axis=1)
    return rep(c), rep(sa), rep(sb)


def _group_layout(route, counts, tm):
    T = route.shape[0]
    counts = counts[0, :N_EXPERTS].astype(jnp.int32)
    n_tile_e = (counts + tm - 1) // tm
    tile_end = jnp.cumsum(n_tile_e)
    start = (tile_end - n_tile_e) * tm
    expert = route[:, 0:TOP_K].astype(jnp.int32)
    rank = route[:, 4:4 + TOP_K].astype(jnp.int32)
    onehot = expert[..., None] == jnp.arange(N_EXPERTS, dtype=jnp.int32)
    dst = jnp.sum(jnp.where(onehot, start, 0), axis=-1) + rank
    n_tiles = TOP_K * T // tm + N_EXPERTS
    tile = jnp.arange(n_tiles, dtype=jnp.int32)
    tile_expert = jnp.minimum(
        jnp.sum((tile[:, None] >= tile_end[None, :]).astype(jnp.int32), axis=1), N_EXPERTS - 1)
    own = tile_expert[:, None] == jnp.arange(N_EXPERTS, dtype=jnp.int32)[None, :]
    left = jnp.sum(jnp.where(own, counts - (tile[:, None] - (tile_end - n_tile_e)) * tm, 0), axis=1)
    n_valid = jnp.where(tile < tile_end[-1], jnp.clip(left, 0, tm), 0).astype(jnp.int32)
    return dst, tile_expert, n_valid, n_tiles * tm


def _trunk(x, p, W):
    B, S, _ = x.shape
    T = B * S
    depth = p.shape[0]
    x = x.reshape(T, D_MODEL)
    p = p.reshape(depth, T, PLE_DIM)
    cos, sa, sb = _rope_tables(S)
    for i in range(depth):
        lam_init = 0.8 - 0.6 * math.exp(-0.3 * i)
        q, k, v, gm = _inproj(x, W["norm_mix"][i], W["w_in"][i], cos, sa, sb, W["gmlp_ln_g"][i],
                              W["gmlp_ln_b"][i], W["gmlp_ws"][i], W["gmlp_bs"][i], S)
        att = _attention(q, k, v, W["lam"][i], W["subln"][i], B, S, lam_init)
        moe = i % 2 == 1
        j = i // 2
        fin = W["final_norm"] if i == depth - 1 else None
        ple_w = (W["ple_gate_w"][i], W["ple_gate_b"][i], W["ple_w"][i], W["ple_norm"][i], fin)
        if not moe:
            x1, h2 = _outproj(att, gm, x, W["w_out"][i], W["norm_ffn"][i], None)
            x2 = _dense_ffn(h2, x1, W["dense_w_gate"], W["dense_w_up"], W["dense_w_down"], j)
            x = _ple(x2, None, None, None, p, i, *ple_w)
        else:
            x1, hp, route, counts = _outproj(att, gm, x, W["w_out"][i], W["norm_ffn"][i],
                                             W["router"][j])
            tm = _tile(T, MOE_TM)
            dst, tile_expert, n_valid, n_rows = _group_layout(route, counts, tm)
            xs = _sc_scatter_rows(hp, dst, n_rows)
            ys = _moe_ffn(xs, tile_expert, n_valid, W["moe_w_gate"], W["moe_w_up"],
                          W["moe_w_down"], j, tm)
            x = _ple(x1, dst, ys, route, p, i, *ple_w)
    return x.reshape(B, S, D_MODEL)


def kernel(x_prompt, x_sample, p_prompt, p_sample, norm_mix, w_in, lambda_q1, lambda_k1, lambda_q2, lambda_k2, subln, gmlp_ln_g, gmlp_ln_b, gmlp_ws, gmlp_bs, w_out, norm_ffn, dense_w_gate, dense_w_up, dense_w_down, router, moe_w_gate, moe_w_up, moe_w_down, ple_w, ple_norm, ple_gate_w, ple_gate_b, final_norm):
    depth = w_in.shape[0]
    row = lambda t: t[:, None, :]
    lam = jnp.stack([lambda_q1, lambda_k1, lambda_q2, lambda_k2], axis=1)
    lam = jnp.pad(lam, ((0, 0), (0, 4), (0, LANES - HEAD_DIM)))
    r_hi = router.astype(BF16)
    r_lo = (router - r_hi.astype(F32)).astype(BF16)
    W = dict(
        norm_mix=row(norm_mix), w_in=w_in.astype(BF16), lam=lam, subln=row(subln),
        gmlp_ln_g=row(gmlp_ln_g), gmlp_ln_b=row(gmlp_ln_b), gmlp_ws=gmlp_ws.astype(BF16),
        gmlp_bs=jnp.broadcast_to(gmlp_bs[..., None], gmlp_bs.shape + (CHUNK,)),
        w_out=w_out.astype(BF16), norm_ffn=row(norm_ffn),
        dense_w_gate=dense_w_gate.astype(BF16), dense_w_up=dense_w_up.astype(BF16),
        dense_w_down=dense_w_down.astype(BF16),
        router=jnp.pad(jnp.concatenate([r_hi, r_lo], axis=-1),
                       ((0, 0), (0, 0), (0, LANES - 2 * N_EXPERTS))),
        moe_w_gate=moe_w_gate.astype(BF16), moe_w_up=moe_w_up.astype(BF16),
        moe_w_down=moe_w_down.astype(BF16),
        ple_w=ple_w.astype(BF16), ple_norm=row(ple_norm), ple_gate_w=ple_gate_w.astype(BF16),
        ple_gate_b=row(ple_gate_b), final_norm=final_norm[None, :],
    )
    return (_trunk(x_prompt, p_prompt, W), _trunk(x_sample, p_sample, W))
```

```python
import functools
import math

import jax
import jax.numpy as jnp
from jax import lax
from jax.experimental import pallas as pl
from jax.experimental.pallas import tpu as pltpu
from jax.experimental.pallas import tpu_sc as plsc

F32 = jnp.float32
BF16 = jnp.bfloat16

D_MODEL = 1024
ATTN_WIDTH = 512
GMLP_WIDTH = 512
HEAD_DIM = 64
HEAD_WIDTH = 2 * HEAD_DIM
N_HEADS = ATTN_WIDTH // HEAD_WIDTH
ROT_DIM = HEAD_DIM // 4
ROPE_THETA = 500000.0
CHUNK = 128
N_GROUPS = 4
D_FF = 3584
N_EXPERTS = 8
TOP_K = 2
PLE_DIM = 256
EPS = 1e-6
LANES = 128
Q_SCALE = (1.0 / math.sqrt(HEAD_DIM)) * math.log2(math.e)
VMEM_LIMIT = 48 * 1024 * 1024
ATTN_TQ = 1024
ATTN_TK = 1024
ATTN_MIN_STEPS = 8
ATTN_COLS = 256
FFN_TM = 512
MOE_TM = 512
FFN_ROW_SPLIT = 2
SC_WINDOW = 32
SC_SCATTER_WINDOW = 128
ROW_TM = 1024
ROW_SPLIT = 1


def _tile(n, pref):
    t = min(n, pref)
    while n % t:
        t //= 2
    return t


def _params(sem, **kw):
    return pltpu.CompilerParams(dimension_semantics=sem, vmem_limit_bytes=VMEM_LIMIT, **kw)


def _rms(x):
    return x * lax.rsqrt(jnp.mean(x * x, axis=-1, keepdims=True) + EPS)


def _inproj_kernel(x_ref, g_ref, w_ref, cos_ref, sa_ref, sb_ref, lng_ref, lnb_ref, ws_ref, bs_ref,
                   q_ref, k_ref, v_ref, gm_ref):
    tm = x_ref.shape[0]
    rb = tm // ROW_SPLIT
    half = ROT_DIM // 2
    for s in range(ROW_SPLIT):
        rows = slice(s * rb, (s + 1) * rb)
        h = (_rms(x_ref[rows, :]) * g_ref[...]).astype(BF16)

        def proj(c0, n):
            return jnp.dot(h, w_ref[:, c0:c0 + n], preferred_element_type=F32)

        cos, sa, sb = cos_ref[rows, :], sa_ref[rows, :], sb_ref[rows, :]

        def rope(t):
            outs = []
            for hh in range(N_HEADS):
                th = t[:, hh * HEAD_WIDTH:(hh + 1) * HEAD_WIDTH]
                outs.append(th * cos + pltpu.roll(th, half, 1) * sa
                            + pltpu.roll(th, HEAD_WIDTH - half, 1) * sb)
            return jnp.concatenate(outs, axis=1)

        q_ref[rows, :] = (rope(proj(0, ATTN_WIDTH)) * Q_SCALE).astype(BF16)
        k_ref[rows, :] = rope(proj(ATTN_WIDTH, ATTN_WIDTH)).astype(BF16)
        v_ref[rows, :] = proj(2 * ATTN_WIDTH, ATTN_WIDTH).astype(BF16)

        u = jax.nn.gelu(proj(3 * ATTN_WIDTH, GMLP_WIDTH))
        vf = jax.nn.gelu(proj(3 * ATTN_WIDTH + GMLP_WIDTH, GMLP_WIDTH))
        mu = jnp.mean(vf, axis=-1, keepdims=True)
        vc = vf - mu
        var = jnp.mean(vc * vc, axis=-1, keepdims=True)
        vn = (vc * lax.rsqrt(var + EPS) * lng_ref[...] + lnb_ref[...]).astype(BF16)
        for c in range(rb // CHUNK):
            r = slice(c * CHUNK, (c + 1) * CHUNK)
            ro = slice(s * rb + c * CHUNK, s * rb + (c + 1) * CHUNK)
            for g in range(N_GROUPS):
                cg = slice(g * CHUNK, (g + 1) * CHUNK)
                mixed = jnp.dot(ws_ref[g], vn[r, cg], preferred_element_type=F32) + bs_ref[g]
                gm_ref[ro, cg] = (u[r, cg] * mixed).astype(BF16)


def _inproj(x, g, w, cos, sa, sb, lng, lnb, ws, bsb, seq):
    T = x.shape[0]
    tm = _tile(seq, ROW_TM)
    ns = seq // tm
    full = lambda i: (0, 0)
    out = jax.ShapeDtypeStruct((T, ATTN_WIDTH), BF16)
    return pl.pallas_call(
        _inproj_kernel,
        grid=(T // tm,),
        in_specs=[
            pl.BlockSpec((tm, D_MODEL), lambda i: (i, 0)),
            pl.BlockSpec((1, D_MODEL), full),
            pl.BlockSpec(w.shape, full),
            pl.BlockSpec((tm, HEAD_WIDTH), lambda i: (i % ns, 0)),
            pl.BlockSpec((tm, HEAD_WIDTH), lambda i: (i % ns, 0)),
            pl.BlockSpec((tm, HEAD_WIDTH), lambda i: (i % ns, 0)),
            pl.BlockSpec((1, GMLP_WIDTH), full),
            pl.BlockSpec((1, GMLP_WIDTH), full),
            pl.BlockSpec(ws.shape, lambda i: (0, 0, 0)),
            pl.BlockSpec(bsb.shape, lambda i: (0, 0, 0)),
        ],
        out_specs=[pl.BlockSpec((tm, ATTN_WIDTH), lambda i: (i, 0))] * 4,
        out_shape=[out] * 4,
        compiler_params=_params(("parallel",)),
        name="inproj",
    )(x, g, w, cos, sa, sb, lng, lnb, ws, bsb)


def _attn_kernel(q_ref, k_ref, v_ref, lam_ref, sub_ref, o_ref,
                 qcat_ref, sa_ref, sb_ref, m_ref, l_ref, acc_ref, *, tk, lam_init):
    tq = q_ref.shape[0]
    n = k_ref.shape[0] // tk
    qt = q_ref[...].astype(F32).T
    row = lax.broadcasted_iota(jnp.int32, qt.shape, 0)
    qcat_ref[:, :tq] = jnp.where(row < HEAD_DIM, qt, 0.0).astype(BF16)
    qcat_ref[:, tq:] = jnp.where(row >= HEAD_DIM, qt, 0.0).astype(BF16)
    m_ref[...] = jnp.full(m_ref.shape, -jnp.inf, F32)
    l_ref[...] = jnp.zeros(l_ref.shape, F32)
    acc_ref[...] = jnp.zeros(acc_ref.shape, F32)

    def rows(t):
        return pl.ds(pl.multiple_of(t * tk, tk), tk)

    def scores(t, s_ref):
        s_ref[...] = jnp.dot(k_ref[rows(t), :], qcat_ref[...], preferred_element_type=F32)

    def update(t, s_ref):
        vt = v_ref[rows(t), :]
        for cb in range(2 * tq // ATTN_COLS):
            cols = slice(cb * ATTN_COLS, (cb + 1) * ATTN_COLS)
            s = s_ref[:, cols]
            m_old = m_ref[:, cols]
            m_new = jnp.maximum(m_old, jnp.max(s, axis=0, keepdims=True))
            alpha = jnp.exp2(m_old - m_new)
            p = jnp.exp2(s - m_new)
            l_ref[:, cols] = alpha * l_ref[:, cols] + jnp.sum(p, axis=0, keepdims=True)
            m_ref[:, cols] = m_new
            pv = lax.dot_general(vt, p.astype(BF16), (((0,), (0,)), ((), ())),
                                 preferred_element_type=F32)
            acc_ref[:, cols] = alpha * acc_ref[:, cols] + pv

    scores(0, sa_ref)

    def pair(u, carry):
        t = 2 * u
        scores(t + 1, sb_ref)
        update(t, sa_ref)
        scores(t + 2, sa_ref)
        update(t + 1, sb_ref)
        return carry

    lax.fori_loop(0, n // 2 - 1, pair, 0)
    scores(n - 1, sb_ref)
    update(n - 2, sa_ref)
    update(n - 1, sb_ref)

    lp = lam_ref[...]
    lam = (jnp.exp(jnp.sum(lp[0:1] * lp[1:2], axis=-1, keepdims=True))
           - jnp.exp(jnp.sum(lp[2:3] * lp[3:4], axis=-1, keepdims=True)) + lam_init)
    l = l_ref[...]
    ot = acc_ref[:, :tq] / l[:, :tq] - lam * (acc_ref[:, tq:] / l[:, tq:])
    o = ot.T
    o_ref[...] = (_rms(o) * sub_ref[...] * (1.0 - lam_init)).astype(o_ref.dtype)


def _attention(q, k, v, lam_p, sub, batch, seq, lam_init):
    T = q.shape[0]
    tq = _tile(seq, ATTN_TQ)
    tk = _tile(seq // ATTN_MIN_STEPS, ATTN_TK)
    nq = seq // tq
    kv_spec = pl.BlockSpec((seq, HEAD_WIDTH), lambda b, h, i: (b, h))
    return pl.pallas_call(
        functools.partial(_attn_kernel, tk=tk, lam_init=lam_init),
        grid=(batch, N_HEADS, nq),
        in_specs=[
            pl.BlockSpec((tq, HEAD_WIDTH), lambda b, h, i: (b * nq + i, h)),
            kv_spec,
            kv_spec,
            pl.BlockSpec(lam_p.shape, lambda b, h, i: (0, 0)),
            pl.BlockSpec((1, HEAD_WIDTH), lambda b, h, i: (0, 0)),
        ],
        out_specs=pl.BlockSpec((tq, HEAD_WIDTH), lambda b, h, i: (b * nq + i, h)),
        out_shape=jax.ShapeDtypeStruct((T, ATTN_WIDTH), BF16),
        scratch_shapes=[
            pltpu.VMEM((HEAD_WIDTH, 2 * tq), BF16),
            pltpu.VMEM((tk, 2 * tq), F32),
            pltpu.VMEM((tk, 2 * tq), F32),
            pltpu.VMEM((1, 2 * tq), F32),
            pltpu.VMEM((1, 2 * tq), F32),
            pltpu.VMEM((HEAD_WIDTH, 2 * tq), F32),
        ],
        compiler_params=_params(("parallel", "parallel", "parallel")),
        name="attn",
    )(q, k, v, lam_p, sub)


def _pack_bf16_pairs(h):
    half = D_MODEL // 2
    h16 = h.astype(BF16)
    lo = lax.bitcast_convert_type(h16[:, :half].astype(F32), jnp.uint32)
    hi = lax.bitcast_convert_type(h16[:, half:].astype(F32), jnp.uint32)
    return (lo >> 16) | (hi & jnp.uint32(0xFFFF0000))


def _unpack_bf16_pairs(xp):
    lo = lax.bitcast_convert_type(xp << 16, F32)
    hi = lax.bitcast_convert_type(xp & jnp.uint32(0xFFFF0000), F32)
    return lo.astype(BF16), hi.astype(BF16)


def _outproj_kernel(att_ref, gm_ref, x_ref, w_ref, g_ref, *rest, moe):
    if moe:
        r_ref, x1_ref, h_ref, route_ref, count_ref, cnt_ref = rest
    else:
        x1_ref, h_ref = rest
    y = (jnp.dot(att_ref[...], w_ref[:ATTN_WIDTH, :], preferred_element_type=F32)
         + jnp.dot(gm_ref[...], w_ref[ATTN_WIDTH:, :], preferred_element_type=F32))
    x1 = x_ref[...] + y
    x1_ref[...] = x1
    h = _rms(x1) * g_ref[...]
    if not moe:
        h_ref[...] = h.astype(BF16)
        return
    h_ref[...] = _pack_bf16_pairs(h)
    tm = h.shape[0]
    h_hi = h.astype(BF16)
    h_lo = (h - h_hi.astype(F32)).astype(BF16)
    part = (jnp.dot(h_hi, r_ref[...], preferred_element_type=F32)
            + jnp.dot(h_lo, r_ref[...], preferred_element_type=F32))
    logits = part + pltpu.roll(part, LANES - N_EXPERTS, 1)
    lane = lax.broadcasted_iota(jnp.int32, logits.shape, 1)
    lg = jnp.where(lane < N_EXPERTS, logits, -jnp.inf)
    v1 = jnp.max(lg, axis=-1, keepdims=True)
    i1 = jnp.min(jnp.where(lg == v1, lane, LANES), axis=-1, keepdims=True)
    lg2 = jnp.where(lane == i1, -jnp.inf, lg)
    v2 = jnp.max(lg2, axis=-1, keepdims=True)
    i2 = jnp.min(jnp.where(lg2 == v2, lane, LANES), axis=-1, keepdims=True)
    e2 = jnp.exp(v2 - v1)
    den = 1.0 + e2

    @pl.when(pl.program_id(0) == 0)
    def _():
        cnt_ref[...] = jnp.zeros(cnt_ref.shape, F32)

    sel1 = lane == i1
    sel2 = lane == i2
    onehot = jnp.where(jnp.logical_or(sel1, sel2), 1.0, 0.0)
    r_i = lax.broadcasted_iota(jnp.int32, (tm, tm), 0)
    c_i = lax.broadcasted_iota(jnp.int32, (tm, tm), 1)
    tri = jnp.where(c_i < r_i, 1.0, 0.0).astype(BF16)
    before = jnp.dot(tri, onehot.astype(BF16), preferred_element_type=F32) + cnt_ref[...]
    rank1 = jnp.sum(jnp.where(sel1, before, 0.0), axis=-1, keepdims=True)
    rank2 = jnp.sum(jnp.where(sel2, before, 0.0), axis=-1, keepdims=True)
    cnt = cnt_ref[...] + jnp.sum(onehot, axis=0, keepdims=True)
    cnt_ref[...] = cnt
    count_ref[...] = jnp.broadcast_to(cnt, count_ref.shape)
    route = jnp.zeros(logits.shape, F32)
    for col, val in enumerate((i1.astype(F32), i2.astype(F32), 1.0 / den, e2 / den, rank1, rank2)):
        route = jnp.where(lane == col, val, route)
    route_ref[...] = route


def _outproj(att, gm, x, w, g, router_pad):
    T = x.shape[0]
    moe = router_pad is not None
    tm = _tile(T, ROW_TM // 2 if moe else ROW_TM)
    full = lambda i: (0, 0)
    row = lambda i: (i, 0)
    in_specs = [
        pl.BlockSpec((tm, ATTN_WIDTH), row),
        pl.BlockSpec((tm, GMLP_WIDTH), row),
        pl.BlockSpec((tm, D_MODEL), row),
        pl.BlockSpec(w.shape, full),
        pl.BlockSpec((1, D_MODEL), full),
    ]
    args = [att, gm, x, w, g]
    if not moe:
        out_specs = [pl.BlockSpec((tm, D_MODEL), row), pl.BlockSpec((tm, D_MODEL), row)]
        out_shape = [jax.ShapeDtypeStruct((T, D_MODEL), F32),
                     jax.ShapeDtypeStruct((T, D_MODEL), BF16)]
        scratch = []
    else:
        in_specs.append(pl.BlockSpec(router_pad.shape, full))
        args.append(router_pad)
        out_specs = [pl.BlockSpec((tm, D_MODEL), row), pl.BlockSpec((tm, D_MODEL // 2), row),
                     pl.BlockSpec((tm, LANES), row), pl.BlockSpec((8, LANES), full)]
        out_shape = [jax.ShapeDtypeStruct((T, D_MODEL), F32),
                     jax.ShapeDtypeStruct((T, D_MODEL // 2), jnp.uint32),
                     jax.ShapeDtypeStruct((T, LANES), F32),
                     jax.ShapeDtypeStruct((8, LANES), F32)]
        scratch = [pltpu.VMEM((1, LANES), F32)]
    return pl.pallas_call(
        functools.partial(_outproj_kernel, moe=moe),
        grid=(T // tm,),
        in_specs=in_specs,
        out_specs=out_specs,
        out_shape=out_shape,
        scratch_shapes=scratch,
        compiler_params=_params(("arbitrary",) if moe else ("parallel",)),
        name="outproj_moe" if moe else "outproj",
    )(*args)


def _sc_scatter_rows(src, dst2, n_rows):
    n, width = src.shape
    info = plsc.get_sparse_core_info()
    n_workers = info.num_cores * info.num_subcores
    per_worker = n // n_workers
    win = SC_SCATTER_WINDOW
    idx = [dst2[:, k].reshape(n // win, win) for k in range(TOP_K)]
    mesh = plsc.VectorSubcoreMesh(core_axis_name="c", subcore_axis_name="s")

    @functools.partial(
        pl.kernel, mesh=mesh,
        out_type=jax.ShapeDtypeStruct((n_rows, width), src.dtype),
        scratch_types=[pltpu.VMEM((1, win), jnp.int32), pltpu.VMEM((1, win), jnp.int32),
                       pltpu.VMEM((win, width), src.dtype)])
    def scatter(src_hbm, i0_hbm, i1_hbm, out_hbm, i0_v, i1_v, rows_v):
        wid = lax.axis_index("s") * info.num_cores + lax.axis_index("c")
        base = wid * per_worker

        @pl.loop(0, per_worker, step=win)
        def _(off):
            w = (base + off) // win
            pltpu.sync_copy(i0_hbm.at[pl.ds(w, 1)], i0_v)
            pltpu.sync_copy(i1_hbm.at[pl.ds(w, 1)], i1_v)
            pltpu.sync_copy(src_hbm.at[pl.ds(base + off, win)], rows_v)
            pltpu.sync_copy(rows_v, out_hbm.at[i0_v.at[0]])
            pltpu.sync_copy(rows_v, out_hbm.at[i1_v.at[0]])

    return scatter(src, *idx)


def _swiglu_step(h_ref, wg_ref, wu_ref, wd_ref, acc_ref):
    tm = h_ref.shape[0]
    rb = tm // FFN_ROW_SPLIT

    @pl.when(pl.program_id(1) == 0)
    def _():
        acc_ref[...] = jnp.zeros(acc_ref.shape, F32)

    for s in range(FFN_ROW_SPLIT):
        rows = slice(s * rb, (s + 1) * rb)
        h = h_ref[rows, :]
        a = (jax.nn.silu(jnp.dot(h, wg_ref[...], preferred_element_type=F32))
             * jnp.dot(h, wu_ref[...], preferred_element_type=F32)).astype(BF16)
        acc_ref[rows, :] += jnp.dot(a, wd_ref[...], preferred_element_type=F32)


def _dense_ffn_kernel(h_ref, x_ref, wg_ref, wu_ref, wd_ref, o_ref, acc_ref):
    _swiglu_step(h_ref, wg_ref, wu_ref, wd_ref, acc_ref)

    @pl.when(pl.program_id(1) == pl.num_programs(1) - 1)
    def _():
        o_ref[...] = x_ref[...] + acc_ref[...]


def _dense_ffn(h, x, wg, wu, wd, layer):
    T = h.shape[0]
    tm = _tile(T, FFN_TM)
    tf = D_FF // 2 if D_FF % 512 == 0 else _tile(D_FF, 512)
    row = lambda i, j: (i, 0)
    return pl.pallas_call(
        _dense_ffn_kernel,
        grid=(T // tm, D_FF // tf),
        in_specs=[
            pl.BlockSpec((tm, D_MODEL), row),
            pl.BlockSpec((tm, D_MODEL), row),
            pl.BlockSpec((None, D_MODEL, tf), lambda i, j: (layer, 0, j)),
            pl.BlockSpec((None, D_MODEL, tf), lambda i, j: (layer, 0, j)),
            pl.BlockSpec((None, tf, D_MODEL), lambda i, j: (layer, j, 0)),
        ],
        out_specs=pl.BlockSpec((tm, D_MODEL), row),
        out_shape=jax.ShapeDtypeStruct((T, D_MODEL), F32),
        scratch_shapes=[pltpu.VMEM((tm, D_MODEL), F32)],
        compiler_params=_params(("parallel", "arbitrary")),
        name="dense_ffn",
    )(h, x, wg, wu, wd)


def _moe_ffn_kernel(te_ref, nv_ref, xp_ref, wg_ref, wu_ref, wd_ref, o_ref, acc_ref, h_ref):
    n_valid = nv_ref[pl.program_id(0)]
    last = pl.program_id(1) == pl.num_programs(1) - 1

    @pl.when(n_valid > 0)
    def _():
        @pl.when(pl.program_id(1) == 0)
        def _():
            lo, hi = _unpack_bf16_pairs(xp_ref[...])
            live = lax.broadcasted_iota(jnp.int32, lo.shape, 0) < n_valid
            h_ref[:, :D_MODEL // 2] = jnp.where(live, lo, jnp.zeros_like(lo))
            h_ref[:, D_MODEL // 2:] = jnp.where(live, hi, jnp.zeros_like(hi))

        _swiglu_step(h_ref, wg_ref, wu_ref, wd_ref, acc_ref)

        @pl.when(last)
        def _():
            o_ref[...] = acc_ref[...]

    @pl.when(jnp.logical_and(n_valid == 0, last))
    def _():
        o_ref[...] = jnp.zeros(o_ref.shape, o_ref.dtype)


def _moe_ffn(xs, tile_expert, n_valid, wg, wu, wd, layer, tm):
    P = xs.shape[0]
    tf = D_FF // 2 if D_FF % 512 == 0 else _tile(D_FF, 512)
    row = lambda i, j, te, nu: (i, 0)
    up = lambda i, j, te, nu: (layer, te[i], 0, j)
    return pl.pallas_call(
        _moe_ffn_kernel,
        grid_spec=pltpu.PrefetchScalarGridSpec(
            num_scalar_prefetch=2,
            grid=(P // tm, D_FF // tf),
            in_specs=[
                pl.BlockSpec((tm, D_MODEL // 2), row),
                pl.BlockSpec((None, None, D_MODEL, tf), up),
                pl.BlockSpec((None, None, D_MODEL, tf), up),
                pl.BlockSpec((None, None, tf, D_MODEL), lambda i, j, te, nu: (layer, te[i], j, 0)),
            ],
            out_specs=pl.BlockSpec((tm, D_MODEL), row),
            scratch_shapes=[pltpu.VMEM((tm, D_MODEL), F32), pltpu.VMEM((tm, D_MODEL), BF16)],
        ),
        out_shape=jax.ShapeDtypeStruct((P, D_MODEL), F32),
        compiler_params=_params(("arbitrary", "arbitrary")),
        name="moe_ffn",
    )(tile_expert, n_valid, xs, wg, wu, wd)


def _sc_gather_rows(src, idx):
    n = idx.shape[0]
    info = plsc.get_sparse_core_info()
    n_workers = info.num_cores * info.num_subcores
    per_worker = n // n_workers
    mesh = plsc.VectorSubcoreMesh(core_axis_name="c", subcore_axis_name="s")

    @functools.partial(
        pl.kernel, mesh=mesh,
        out_type=jax.ShapeDtypeStruct((n, D_MODEL), src.dtype),
        scratch_types=[pltpu.VMEM((SC_WINDOW,), jnp.int32),
                       pltpu.VMEM((SC_WINDOW, D_MODEL), src.dtype),
                       pltpu.SemaphoreType.DMA])
    def gather(src_hbm, idx_hbm, out_hbm, idx_v, rows_v, sem):
        wid = lax.axis_index("s") * info.num_cores + lax.axis_index("c")
        base = wid * per_worker

        @pl.loop(0, per_worker, step=SC_WINDOW)
        def _(off):
            pltpu.sync_copy(idx_hbm.at[pl.ds(base + off, SC_WINDOW)], idx_v)
            pltpu.async_copy(src_hbm.at[idx_v], rows_v, sem).wait()
            pltpu.sync_copy(rows_v, out_hbm.at[pl.ds(base + off, SC_WINDOW)])

    return gather(src, idx)


def _ple_kernel(*refs, moe, last):
    refs = list(refs)
    x_ref = refs.pop(0)
    if moe:
        y0_ref, y1_ref, route_ref = refs.pop(0), refs.pop(0), refs.pop(0)
    p_ref, wg_ref, b_ref, pw_ref, pn_ref = refs[:5]
    refs = refs[5:]
    if last:
        fn_ref = refs.pop(0)
    o_ref = refs.pop(0)

    e = jnp.dot(p_ref[...].astype(BF16), pw_ref[...], preferred_element_type=F32)
    e = _rms(e) * pn_ref[...]
    x = x_ref[...]
    if moe:
        rt = route_ref[...]
        x = x + (rt[:, 2:3] * y0_ref[...] + rt[:, 3:4] * y1_ref[...])
    gate = jax.nn.sigmoid(jnp.dot(_rms(x).astype(BF16), wg_ref[...], preferred_element_type=F32)
                          + b_ref[...])
    x = x + gate * e
    if last:
        x = _rms(x) * fn_ref[...]
    o_ref[...] = x


def _ple(x, dst, ys, route, p, layer, wg, b, pw, pn, final_norm):
    T = x.shape[0]
    moe = ys is not None
    last = final_norm is not None
    tm = _tile(T, ROW_TM // 2 if moe else ROW_TM)
    full = lambda i: (0, 0)
    row = lambda i: (i, 0)
    in_specs = [pl.BlockSpec((tm, D_MODEL), row)]
    args = [x]
    scratch = []
    if moe:
        in_specs += [pl.BlockSpec((tm, D_MODEL), row), pl.BlockSpec((tm, D_MODEL), row),
                     pl.BlockSpec((tm, LANES), row)]
        args += [_sc_gather_rows(ys, dst[:, 0]), _sc_gather_rows(ys, dst[:, 1]), route]
    in_specs += [
        pl.BlockSpec((None, tm, PLE_DIM), lambda i: (layer, i, 0)),
        pl.BlockSpec(wg.shape, full),
        pl.BlockSpec((1, D_MODEL), full),
        pl.BlockSpec(pw.shape, full),
        pl.BlockSpec((1, D_MODEL), full),
    ]
    args += [p, wg, b, pw, pn]
    if last:
        in_specs.append(pl.BlockSpec((1, D_MODEL), full))
        args.append(final_norm)
    return pl.pallas_call(
        functools.partial(_ple_kernel, moe=moe, last=last),
        grid=(T // tm,),
        in_specs=in_specs,
        out_specs=pl.BlockSpec((tm, D_MODEL), row),
        out_shape=jax.ShapeDtypeStruct((T, D_MODEL), F32),
        scratch_shapes=scratch,
        compiler_params=_params(("parallel",)),
        name="ple" + ("_moe" if moe else "") + ("_last" if last else ""),
    )(*args)


def _rope_tables(seq):
    half = ROT_DIM // 2
    pos = jnp.arange(seq, dtype=F32)
    inv = ROPE_THETA ** (-jnp.arange(half, dtype=F32) * 2.0 / ROT_DIM)
    ang = pos[:, None] * inv[None, :]
    cos, sin = jnp.cos(ang), jnp.sin(ang)
    ones = jnp.ones((seq, HEAD_DIM - ROT_DIM), F32)
    zeros = jnp.zeros((seq, HEAD_DIM - ROT_DIM), F32)
    zh = jnp.zeros((seq, half), F32)
    c = jnp.concatenate([cos, cos, ones], axis=1)
    sa = jnp.concatenate([zh, sin, zeros], axis=1)
    sb = jnp.concatenate([-sin, zh, zeros], axis=1)
    rep = lambda t: jnp.concatenate([t, t], axis=1)
    return rep(c), rep(sa), rep(sb)


def _group_layout(route, counts, tm):
    T = route.shape[0]
    counts = counts[0, :N_EXPERTS].astype(jnp.int32)
    n_tile_e = (counts + tm - 1) // tm
    tile_end = jnp.cumsum(n_tile_e)
    start = (tile_end - n_tile_e) * tm
    expert = route[:, 0:TOP_K].astype(jnp.int32)
    rank = route[:, 4:4 + TOP_K].astype(jnp.int32)
    onehot = expert[..., None] == jnp.arange(N_EXPERTS, dtype=jnp.int32)
    dst = jnp.sum(jnp.where(onehot, start, 0), axis=-1) + rank
    n_tiles = TOP_K * T // tm + N_EXPERTS
    tile = jnp.arange(n_tiles, dtype=jnp.int32)
    tile_expert = jnp.minimum(
        jnp.sum((tile[:, None] >= tile_end[None, :]).astype(jnp.int32), axis=1), N_EXPERTS - 1)
    own = tile_expert[:, None] == jnp.arange(N_EXPERTS, dtype=jnp.int32)[None, :]
    left = jnp.sum(jnp.where(own, counts - (tile[:, None] - (tile_end - n_tile_e)) * tm, 0), axis=1)
    n_valid = jnp.where(tile < tile_end[-1], jnp.clip(left, 0, tm), 0).astype(jnp.int32)
    return dst, tile_expert, n_valid, n_tiles * tm


def _trunk(x, p, W):
    B, S, _ = x.shape
    T = B * S
    depth = p.shape[0]
    x = x.reshape(T, D_MODEL)
    p = p.reshape(depth, T, PLE_DIM)
    cos, sa, sb = _rope_tables(S)
    for i in range(depth):
        lam_init = 0.8 - 0.6 * math.exp(-0.3 * i)
        q, k, v, gm = _inproj(x, W["norm_mix"][i], W["w_in"][i], cos, sa, sb, W["gmlp_ln_g"][i],
                              W["gmlp_ln_b"][i], W["gmlp_ws"][i], W["gmlp_bs"][i], S)
        att = _attention(q, k, v, W["lam"][i], W["subln"][i], B, S, lam_init)
        moe = i % 2 == 1
        j = i // 2
        fin = W["final_norm"] if i == depth - 1 else None
        ple_w = (W["ple_gate_w"][i], W["ple_gate_b"][i], W["ple_w"][i], W["ple_norm"][i], fin)
        if not moe:
            x1, h2 = _outproj(att, gm, x, W["w_out"][i], W["norm_ffn"][i], None)
            x2 = _dense_ffn(h2, x1, W["dense_w_gate"], W["dense_w_up"], W["dense_w_down"], j)
            x = _ple(x2, None, None, None, p, i, *ple_w)
        else:
            x1, hp, route, counts = _outproj(att, gm, x, W["w_out"][i], W["norm_ffn"][i],
                                             W["router"][j])
            tm = _tile(T, MOE_TM)
            dst, tile_expert, n_valid, n_rows = _group_layout(route, counts, tm)
            xs = _sc_scatter_rows(hp, dst, n_rows)
            ys = _moe_ffn(xs, tile_expert, n_valid, W["moe_w_gate"], W["moe_w_up"],
                          W["moe_w_down"], j, tm)
            x = _ple(x1, dst, ys, route, p, i, *ple_w)
    return x.reshape(B, S, D_MODEL)


def kernel(x_prompt, x_sample, p_prompt, p_sample, norm_mix, w_in, lambda_q1, lambda_k1, lambda_q2, lambda_k2, subln, gmlp_ln_g, gmlp_ln_b, gmlp_ws, gmlp_bs, w_out, norm_ffn, dense_w_gate, dense_w_up, dense_w_down, router, moe_w_gate, moe_w_up, moe_w_down, ple_w, ple_norm, ple_gate_w, ple_gate_b, final_norm):
    depth = w_in.shape[0]
    row = lambda t: t[:, None, :]
    lam = jnp.stack([lambda_q1, lambda_k1, lambda_q2, lambda_k2], axis=1)
    lam = jnp.pad(lam, ((0, 0), (0, 4), (0, LANES - HEAD_DIM)))
    r_hi = router.astype(BF16)
    r_lo = (router - r_hi.astype(F32)).astype(BF16)
    W = dict(
        norm_mix=row(norm_mix), w_in=w_in.astype(BF16), lam=lam, subln=row(subln),
        gmlp_ln_g=row(gmlp_ln_g), gmlp_ln_b=row(gmlp_ln_b), gmlp_ws=gmlp_ws.astype(BF16),
        gmlp_bs=jnp.broadcast_to(gmlp_bs[..., None], gmlp_bs.shape + (CHUNK,)),
        w_out=w_out.astype(BF16), norm_ffn=row(norm_ffn),
        dense_w_gate=dense_w_gate.astype(BF16), dense_w_up=dense_w_up.astype(BF16),
        dense_w_down=dense_w_down.astype(BF16),
        router=jnp.pad(jnp.concatenate([r_hi, r_lo], axis=-1),
                       ((0, 0), (0, 0), (0, LANES - 2 * N_EXPERTS))),
        moe_w_gate=moe_w_gate.astype(BF16), moe_w_up=moe_w_up.astype(BF16),
        moe_w_down=moe_w_down.astype(BF16),
        ple_w=ple_w.astype(BF16), ple_norm=row(ple_norm), ple_gate_w=ple_gate_w.astype(BF16),
        ple_gate_b=row(ple_gate_b), final_norm=final_norm[None, :],
    )
    return (_trunk(x_prompt, p_prompt, W), _trunk(x_sample, p_sample, W))
```

```python
import functools
import math

import jax
import jax.numpy as jnp
from jax import lax
from jax.experimental import pallas as pl
from jax.experimental.pallas import tpu as pltpu
from jax.experimental.pallas import tpu_sc as plsc

F32 = jnp.float32
BF16 = jnp.bfloat16

D_MODEL = 1024
ATTN_WIDTH = 512
GMLP_WIDTH = 512
HEAD_DIM = 64
HEAD_WIDTH = 2 * HEAD_DIM
N_HEADS = ATTN_WIDTH // HEAD_WIDTH
ROT_DIM = HEAD_DIM // 4
ROPE_THETA = 500000.0
CHUNK = 128
N_GROUPS = 4
D_FF = 3584
N_EXPERTS = 8
TOP_K = 2
PLE_DIM = 256
EPS = 1e-6
LANES = 128
Q_SCALE = (1.0 / math.sqrt(HEAD_DIM)) * math.log2(math.e)
VMEM_LIMIT = 48 * 1024 * 1024
ATTN_TQ = 1024
ATTN_TK = 1024
ATTN_MIN_STEPS = 8
ATTN_COLS = 256
FFN_TM = 512
MOE_TM = 512
FFN_ROW_SPLIT = 2
SC_WINDOW = 64
SC_SCATTER_WINDOW = 128
ROW_TM = 1024
ROW_SPLIT = 1


def _tile(n, pref):
    t = min(n, pref)
    while n % t:
        t //= 2
    return t


def _params(sem, **kw):
    return pltpu.CompilerParams(dimension_semantics=sem, vmem_limit_bytes=VMEM_LIMIT, **kw)


def _rms(x):
    return x * lax.rsqrt(jnp.mean(x * x, axis=-1, keepdims=True) + EPS)


def _inproj_kernel(x_ref, g_ref, w_ref, cos_ref, sa_ref, sb_ref, lng_ref, lnb_ref, ws_ref, bs_ref,
                   q_ref, k_ref, v_ref, gm_ref):
    tm = x_ref.shape[0]
    rb = tm // ROW_SPLIT
    half = ROT_DIM // 2
    for s in range(ROW_SPLIT):
        rows = slice(s * rb, (s + 1) * rb)
        h = (_rms(x_ref[rows, :]) * g_ref[...]).astype(BF16)

        def proj(c0, n):
            return jnp.dot(h, w_ref[:, c0:c0 + n], preferred_element_type=F32)

        cos, sa, sb = cos_ref[rows, :], sa_ref[rows, :], sb_ref[rows, :]

        def rope(t):
            outs = []
            for hh in range(N_HEADS):
                th = t[:, hh * HEAD_WIDTH:(hh + 1) * HEAD_WIDTH]
                outs.append(th * cos + pltpu.roll(th, half, 1) * sa
                            + pltpu.roll(th, HEAD_WIDTH - half, 1) * sb)
            return jnp.concatenate(outs, axis=1)

        q_ref[rows, :] = (rope(proj(0, ATTN_WIDTH)) * Q_SCALE).astype(BF16)
        k_ref[rows, :] = rope(proj(ATTN_WIDTH, ATTN_WIDTH)).astype(BF16)
        v_ref[rows, :] = proj(2 * ATTN_WIDTH, ATTN_WIDTH).astype(BF16)

        u = jax.nn.gelu(proj(3 * ATTN_WIDTH, GMLP_WIDTH))
        vf = jax.nn.gelu(proj(3 * ATTN_WIDTH + GMLP_WIDTH, GMLP_WIDTH))
        mu = jnp.mean(vf, axis=-1, keepdims=True)
        vc = vf - mu
        var = jnp.mean(vc * vc, axis=-1, keepdims=True)
        vn = (vc * lax.rsqrt(var + EPS) * lng_ref[...] + lnb_ref[...]).astype(BF16)
        for c in range(rb // CHUNK):
            r = slice(c * CHUNK, (c + 1) * CHUNK)
            ro = slice(s * rb + c * CHUNK, s * rb + (c + 1) * CHUNK)
            for g in range(N_GROUPS):
                cg = slice(g * CHUNK, (g + 1) * CHUNK)
                mixed = jnp.dot(ws_ref[g], vn[r, cg], preferred_element_type=F32) + bs_ref[g]
                gm_ref[ro, cg] = (u[r, cg] * mixed).astype(BF16)


def _inproj(x, g, w, cos, sa, sb, lng, lnb, ws, bsb, seq):
    T = x.shape[0]
    tm = _tile(seq, ROW_TM)
    ns = seq // tm
    full = lambda i: (0, 0)
    out = jax.ShapeDtypeStruct((T, ATTN_WIDTH), BF16)
    return pl.pallas_call(
        _inproj_kernel,
        grid=(T // tm,),
        in_specs=[
            pl.BlockSpec((tm, D_MODEL), lambda i: (i, 0)),
            pl.BlockSpec((1, D_MODEL), full),
            pl.BlockSpec(w.shape, full),
            pl.BlockSpec((tm, HEAD_WIDTH), lambda i: (i % ns, 0)),
            pl.BlockSpec((tm, HEAD_WIDTH), lambda i: (i % ns, 0)),
            pl.BlockSpec((tm, HEAD_WIDTH), lambda i: (i % ns, 0)),
            pl.BlockSpec((1, GMLP_WIDTH), full),
            pl.BlockSpec((1, GMLP_WIDTH), full),
            pl.BlockSpec(ws.shape, lambda i: (0, 0, 0)),
            pl.BlockSpec(bsb.shape, lambda i: (0, 0, 0)),
        ],
        out_specs=[pl.BlockSpec((tm, ATTN_WIDTH), lambda i: (i, 0))] * 4,
        out_shape=[out] * 4,
        compiler_params=_params(("parallel",)),
        name="inproj",
    )(x, g, w, cos, sa, sb, lng, lnb, ws, bsb)


def _attn_kernel(q_ref, k_ref, v_ref, lam_ref, sub_ref, o_ref,
                 qcat_ref, sa_ref, sb_ref, m_ref, l_ref, acc_ref, *, tk, lam_init):
    tq = q_ref.shape[0]
    n = k_ref.shape[0] // tk
    qt = q_ref[...].astype(F32).T
    row = lax.broadcasted_iota(jnp.int32, qt.shape, 0)
    qcat_ref[:, :tq] = jnp.where(row < HEAD_DIM, qt, 0.0).astype(BF16)
    qcat_ref[:, tq:] = jnp.where(row >= HEAD_DIM, qt, 0.0).astype(BF16)
    m_ref[...] = jnp.full(m_ref.shape, -jnp.inf, F32)
    l_ref[...] = jnp.zeros(l_ref.shape, F32)
    acc_ref[...] = jnp.zeros(acc_ref.shape, F32)

    def rows(t):
        return pl.ds(pl.multiple_of(t * tk, tk), tk)

    def scores(t, s_ref):
        s_ref[...] = jnp.dot(k_ref[rows(t), :], qcat_ref[...], preferred_element_type=F32)

    def update(t, s_ref):
        vt = v_ref[rows(t), :]
        for cb in range(2 * tq // ATTN_COLS):
            cols = slice(cb * ATTN_COLS, (cb + 1) * ATTN_COLS)
            s = s_ref[:, cols]
            m_old = m_ref[:, cols]
            m_new = jnp.maximum(m_old, jnp.max(s, axis=0, keepdims=True))
            alpha = jnp.exp2(m_old - m_new)
            p = jnp.exp2(s - m_new)
            l_ref[:, cols] = alpha * l_ref[:, cols] + jnp.sum(p, axis=0, keepdims=True)
            m_ref[:, cols] = m_new
            pv = lax.dot_general(vt, p.astype(BF16), (((0,), (0,)), ((), ())),
                                 preferred_element_type=F32)
            acc_ref[:, cols] = alpha * acc_ref[:, cols] + pv

    scores(0, sa_ref)

    def pair(u, carry):
        t = 2 * u
        scores(t + 1, sb_ref)
        update(t, sa_ref)
        scores(t + 2, sa_ref)
        update(t + 1, sb_ref)
        return carry

    lax.fori_loop(0, n // 2 - 1, pair, 0)
    scores(n - 1, sb_ref)
    update(n - 2, sa_ref)
    update(n - 1, sb_ref)

    lp = lam_ref[...]
    lam = (jnp.exp(jnp.sum(lp[0:1] * lp[1:2], axis=-1, keepdims=True))
           - jnp.exp(jnp.sum(lp[2:3] * lp[3:4], axis=-1, keepdims=True)) + lam_init)
    l = l_ref[...]
    ot = acc_ref[:, :tq] / l[:, :tq] - lam * (acc_ref[:, tq:] / l[:, tq:])
    o = ot.T
    o_ref[...] = (_rms(o) * sub_ref[...] * (1.0 - lam_init)).astype(o_ref.dtype)


def _attention(q, k, v, lam_p, sub, batch, seq, lam_init):
    T = q.shape[0]
    tq = _tile(seq, ATTN_TQ)
    tk = _tile(seq // ATTN_MIN_STEPS, ATTN_TK)
    nq = seq // tq
    kv_spec = pl.BlockSpec((seq, HEAD_WIDTH), lambda b, h, i: (b, h))
    return pl.pallas_call(
        functools.partial(_attn_kernel, tk=tk, lam_init=lam_init),
        grid=(batch, N_HEADS, nq),
        in_specs=[
            pl.BlockSpec((tq, HEAD_WIDTH), lambda b, h, i: (b * nq + i, h)),
            kv_spec,
            kv_spec,
            pl.BlockSpec(lam_p.shape, lambda b, h, i: (0, 0)),
            pl.BlockSpec((1, HEAD_WIDTH), lambda b, h, i: (0, 0)),
        ],
        out_specs=pl.BlockSpec((tq, HEAD_WIDTH), lambda b, h, i: (b * nq + i, h)),
        out_shape=jax.ShapeDtypeStruct((T, ATTN_WIDTH), BF16),
        scratch_shapes=[
            pltpu.VMEM((HEAD_WIDTH, 2 * tq), BF16),
            pltpu.VMEM((tk, 2 * tq), F32),
            pltpu.VMEM((tk, 2 * tq), F32),
            pltpu.VMEM((1, 2 * tq), F32),
            pltpu.VMEM((1, 2 * tq), F32),
            pltpu.VMEM((HEAD_WIDTH, 2 * tq), F32),
        ],
        compiler_params=_params(("parallel", "parallel", "parallel")),
        name="attn",
    )(q, k, v, lam_p, sub)


def _pack_bf16_pairs(h):
    half = D_MODEL // 2
    h16 = h.astype(BF16)
    lo = lax.bitcast_convert_type(h16[:, :half].astype(F32), jnp.uint32)
    hi = lax.bitcast_convert_type(h16[:, half:].astype(F32), jnp.uint32)
    return (lo >> 16) | (hi & jnp.uint32(0xFFFF0000))


def _unpack_bf16_pairs(xp):
    lo = lax.bitcast_convert_type(xp << 16, F32)
    hi = lax.bitcast_convert_type(xp & jnp.uint32(0xFFFF0000), F32)
    return lo.astype(BF16), hi.astype(BF16)


def _outproj_kernel(att_ref, gm_ref, x_ref, w_ref, g_ref, *rest, moe):
    if moe:
        r_ref, x1_ref, h_ref, route_ref, count_ref, cnt_ref = rest
    else:
        x1_ref, h_ref = rest
    y = (jnp.dot(att_ref[...], w_ref[:ATTN_WIDTH, :], preferred_element_type=F32)
         + jnp.dot(gm_ref[...], w_ref[ATTN_WIDTH:, :], preferred_element_type=F32))
    x1 = x_ref[...] + y
    x1_ref[...] = x1
    h = _rms(x1) * g_ref[...]
    if not moe:
        h_ref[...] = h.astype(BF16)
        return
    h_ref[...] = _pack_bf16_pairs(h)
    tm = h.shape[0]
    h_hi = h.astype(BF16)
    h_lo = (h - h_hi.astype(F32)).astype(BF16)
    part = (jnp.dot(h_hi, r_ref[...], preferred_element_type=F32)
            + jnp.dot(h_lo, r_ref[...], preferred_element_type=F32))
    logits = part + pltpu.roll(part, LANES - N_EXPERTS, 1)
    lane = lax.broadcasted_iota(jnp.int32, logits.shape, 1)
    lg = jnp.where(lane < N_EXPERTS, logits, -jnp.inf)
    v1 = jnp.max(lg, axis=-1, keepdims=True)
    i1 = jnp.min(jnp.where(lg == v1, lane, LANES), axis=-1, keepdims=True)
    lg2 = jnp.where(lane == i1, -jnp.inf, lg)
    v2 = jnp.max(lg2, axis=-1, keepdims=True)
    i2 = jnp.min(jnp.where(lg2 == v2, lane, LANES), axis=-1, keepdims=True)
    e2 = jnp.exp(v2 - v1)
    den = 1.0 + e2

    @pl.when(pl.program_id(0) == 0)
    def _():
        cnt_ref[...] = jnp.zeros(cnt_ref.shape, F32)

    sel1 = lane == i1
    sel2 = lane == i2
    onehot = jnp.where(jnp.logical_or(sel1, sel2), 1.0, 0.0)
    r_i = lax.broadcasted_iota(jnp.int32, (tm, tm), 0)
    c_i = lax.broadcasted_iota(jnp.int32, (tm, tm), 1)
    tri = jnp.where(c_i < r_i, 1.0, 0.0).astype(BF16)
    before = jnp.dot(tri, onehot.astype(BF16), preferred_element_type=F32) + cnt_ref[...]
    rank1 = jnp.sum(jnp.where(sel1, before, 0.0), axis=-1, keepdims=True)
    rank2 = jnp.sum(jnp.where(sel2, before, 0.0), axis=-1, keepdims=True)
    cnt = cnt_ref[...] + jnp.sum(onehot, axis=0, keepdims=True)
    cnt_ref[...] = cnt
    count_ref[...] = jnp.broadcast_to(cnt, count_ref.shape)
    route = jnp.zeros(logits.shape, F32)
    for col, val in enumerate((i1.astype(F32), i2.astype(F32), 1.0 / den, e2 / den, rank1, rank2)):
        route = jnp.where(lane == col, val, route)
    route_ref[...] = route


def _outproj(att, gm, x, w, g, router_pad):
    T = x.shape[0]
    moe = router_pad is not None
    tm = _tile(T, ROW_TM // 2 if moe else ROW_TM)
    full = lambda i: (0, 0)
    row = lambda i: (i, 0)
    in_specs = [
        pl.BlockSpec((tm, ATTN_WIDTH), row),
        pl.BlockSpec((tm, GMLP_WIDTH), row),
        pl.BlockSpec((tm, D_MODEL), row),
        pl.BlockSpec(w.shape, full),
        pl.BlockSpec((1, D_MODEL), full),
    ]
    args = [att, gm, x, w, g]
    if not moe:
        out_specs = [pl.BlockSpec((tm, D_MODEL), row), pl.BlockSpec((tm, D_MODEL), row)]
        out_shape = [jax.ShapeDtypeStruct((T, D_MODEL), F32),
                     jax.ShapeDtypeStruct((T, D_MODEL), BF16)]
        scratch = []
    else:
        in_specs.append(pl.BlockSpec(router_pad.shape, full))
        args.append(router_pad)
        out_specs = [pl.BlockSpec((tm, D_MODEL), row), pl.BlockSpec((tm, D_MODEL // 2), row),
                     pl.BlockSpec((tm, LANES), row), pl.BlockSpec((8, LANES), full)]
        out_shape = [jax.ShapeDtypeStruct((T, D_MODEL), F32),
                     jax.ShapeDtypeStruct((T, D_MODEL // 2), jnp.uint32),
                     jax.ShapeDtypeStruct((T, LANES), F32),
                     jax.ShapeDtypeStruct((8, LANES), F32)]
        scratch = [pltpu.VMEM((1, LANES), F32)]
    return pl.pallas_call(
        functools.partial(_outproj_kernel, moe=moe),
        grid=(T // tm,),
        in_specs=in_specs,
        out_specs=out_specs,
        out_shape=out_shape,
        scratch_shapes=scratch,
        compiler_params=_params(("arbitrary",) if moe else ("parallel",)),
        name="outproj_moe" if moe else "outproj",
    )(*args)


def _sc_scatter_rows(src, dst2, n_rows):
    n, width = src.shape
    info = plsc.get_sparse_core_info()
    n_workers = info.num_cores * info.num_subcores
    per_worker = n // n_workers
    win = SC_SCATTER_WINDOW
    idx = [dst2[:, k].reshape(n // win, win) for k in range(TOP_K)]
    mesh = plsc.VectorSubcoreMesh(core_axis_name="c", subcore_axis_name="s")

    @functools.partial(
        pl.kernel, mesh=mesh,
        out_type=jax.ShapeDtypeStruct((n_rows, width), src.dtype),
        scratch_types=[pltpu.VMEM((1, win), jnp.int32), pltpu.VMEM((1, win), jnp.int32),
                       pltpu.VMEM((win, width), src.dtype)])
    def scatter(src_hbm, i0_hbm, i1_hbm, out_hbm, i0_v, i1_v, rows_v):
        wid = lax.axis_index("s") * info.num_cores + lax.axis_index("c")
        base = wid * per_worker

        @pl.loop(0, per_worker, step=win)
        def _(off):
            w = (base + off) // win
            pltpu.sync_copy(i0_hbm.at[pl.ds(w, 1)], i0_v)
            pltpu.sync_copy(i1_hbm.at[pl.ds(w, 1)], i1_v)
            pltpu.sync_copy(src_hbm.at[pl.ds(base + off, win)], rows_v)
            pltpu.sync_copy(rows_v, out_hbm.at[i0_v.at[0]])
            pltpu.sync_copy(rows_v, out_hbm.at[i1_v.at[0]])

    return scatter(src, *idx)


def _swiglu_step(h_ref, wg_ref, wu_ref, wd_ref, acc_ref):
    tm = h_ref.shape[0]
    rb = tm // FFN_ROW_SPLIT

    @pl.when(pl.program_id(1) == 0)
    def _():
        acc_ref[...] = jnp.zeros(acc_ref.shape, F32)

    for s in range(FFN_ROW_SPLIT):
        rows = slice(s * rb, (s + 1) * rb)
        h = h_ref[rows, :]
        a = (jax.nn.silu(jnp.dot(h, wg_ref[...], preferred_element_type=F32))
             * jnp.dot(h, wu_ref[...], preferred_element_type=F32)).astype(BF16)
        acc_ref[rows, :] += jnp.dot(a, wd_ref[...], preferred_element_type=F32)


def _dense_ffn_kernel(h_ref, x_ref, wg_ref, wu_ref, wd_ref, o_ref, acc_ref):
    _swiglu_step(h_ref, wg_ref, wu_ref, wd_ref, acc_ref)

    @pl.when(pl.program_id(1) == pl.num_programs(1) - 1)
    def _():
        o_ref[...] = x_ref[...] + acc_ref[...]


def _dense_ffn(h, x, wg, wu, wd, layer):
    T = h.shape[0]
    tm = _tile(T, FFN_TM)
    tf = D_FF // 2 if D_FF % 512 == 0 else _tile(D_FF, 512)
    row = lambda i, j: (i, 0)
    return pl.pallas_call(
        _dense_ffn_kernel,
        grid=(T // tm, D_FF // tf),
        in_specs=[
            pl.BlockSpec((tm, D_MODEL), row),
            pl.BlockSpec((tm, D_MODEL), row),
            pl.BlockSpec((None, D_MODEL, tf), lambda i, j: (layer, 0, j)),
            pl.BlockSpec((None, D_MODEL, tf), lambda i, j: (layer, 0, j)),
            pl.BlockSpec((None, tf, D_MODEL), lambda i, j: (layer, j, 0)),
        ],
        out_specs=pl.BlockSpec((tm, D_MODEL), row),
        out_shape=jax.ShapeDtypeStruct((T, D_MODEL), F32),
        scratch_shapes=[pltpu.VMEM((tm, D_MODEL), F32)],
        compiler_params=_params(("parallel", "arbitrary")),
        name="dense_ffn",
    )(h, x, wg, wu, wd)


def _moe_ffn_kernel(te_ref, nv_ref, xp_ref, wg_ref, wu_ref, wd_ref, o_ref, acc_ref, h_ref):
    n_valid = nv_ref[pl.program_id(0)]
    last = pl.program_id(1) == pl.num_programs(1) - 1

    @pl.when(n_valid > 0)
    def _():
        @pl.when(pl.program_id(1) == 0)
        def _():
            lo, hi = _unpack_bf16_pairs(xp_ref[...])
            live = lax.broadcasted_iota(jnp.int32, lo.shape, 0) < n_valid
            h_ref[:, :D_MODEL // 2] = jnp.where(live, lo, jnp.zeros_like(lo))
            h_ref[:, D_MODEL // 2:] = jnp.where(live, hi, jnp.zeros_like(hi))

        _swiglu_step(h_ref, wg_ref, wu_ref, wd_ref, acc_ref)

        @pl.when(last)
        def _():
            o_ref[...] = acc_ref[...]

    @pl.when(jnp.logical_and(n_valid == 0, last))
    def _():
        o_ref[...] = jnp.zeros(o_ref.shape, o_ref.dtype)


def _moe_ffn(xs, tile_expert, n_valid, wg, wu, wd, layer, tm):
    P = xs.shape[0]
    tf = D_FF // 2 if D_FF % 512 == 0 else _tile(D_FF, 512)
    row = lambda i, j, te, nu: (i, 0)
    up = lambda i, j, te, nu: (layer, te[i], 0, j)
    return pl.pallas_call(
        _moe_ffn_kernel,
        grid_spec=pltpu.PrefetchScalarGridSpec(
            num_scalar_prefetch=2,
            grid=(P // tm, D_FF // tf),
            in_specs=[
                pl.BlockSpec((tm, D_MODEL // 2), row),
                pl.BlockSpec((None, None, D_MODEL, tf), up),
                pl.BlockSpec((None, None, D_MODEL, tf), up),
                pl.BlockSpec((None, None, tf, D_MODEL), lambda i, j, te, nu: (layer, te[i], j, 0)),
            ],
            out_specs=pl.BlockSpec((tm, D_MODEL), row),
            scratch_shapes=[pltpu.VMEM((tm, D_MODEL), F32), pltpu.VMEM((tm, D_MODEL), BF16)],
        ),
        out_shape=jax.ShapeDtypeStruct((P, D_MODEL), F32),
        compiler_params=_params(("arbitrary", "arbitrary")),
        name="moe_ffn",
    )(tile_expert, n_valid, xs, wg, wu, wd)


def _sc_gather_rows(src, idx):
    n = idx.shape[0]
    info = plsc.get_sparse_core_info()
    n_workers = info.num_cores * info.num_subcores
    per_worker = n // n_workers
    mesh = plsc.VectorSubcoreMesh(core_axis_name="c", subcore_axis_name="s")

    @functools.partial(
        pl.kernel, mesh=mesh,
        out_type=jax.ShapeDtypeStruct((n, D_MODEL), src.dtype),
        scratch_types=[pltpu.VMEM((SC_WINDOW,), jnp.int32),
                       pltpu.VMEM((SC_WINDOW, D_MODEL), src.dtype),
                       pltpu.SemaphoreType.DMA])
    def gather(src_hbm, idx_hbm, out_hbm, idx_v, rows_v, sem):
        wid = lax.axis_index("s") * info.num_cores + lax.axis_index("c")
        base = wid * per_worker

        @pl.loop(0, per_worker, step=SC_WINDOW)
        def _(off):
            pltpu.sync_copy(idx_hbm.at[pl.ds(base + off, SC_WINDOW)], idx_v)
            pltpu.async_copy(src_hbm.at[idx_v], rows_v, sem).wait()
            pltpu.sync_copy(rows_v, out_hbm.at[pl.ds(base + off, SC_WINDOW)])

    return gather(src, idx)


def _ple_kernel(*refs, moe, last):
    refs = list(refs)
    x_ref = refs.pop(0)
    if moe:
        y0_ref, y1_ref, route_ref = refs.pop(0), refs.pop(0), refs.pop(0)
    p_ref, wg_ref, b_ref, pw_ref, pn_ref = refs[:5]
    refs = refs[5:]
    if last:
        fn_ref = refs.pop(0)
    o_ref = refs.pop(0)

    e = jnp.dot(p_ref[...].astype(BF16), pw_ref[...], preferred_element_type=F32)
    e = _rms(e) * pn_ref[...]
    x = x_ref[...]
    if moe:
        rt = route_ref[...]
        x = x + (rt[:, 2:3] * y0_ref[...] + rt[:, 3:4] * y1_ref[...])
    gate = jax.nn.sigmoid(jnp.dot(_rms(x).astype(BF16), wg_ref[...], preferred_element_type=F32)
                          + b_ref[...])
    x = x + gate * e
    if last:
        x = _rms(x) * fn_ref[...]
    o_ref[...] = x


def _ple(x, dst, ys, route, p, layer, wg, b, pw, pn, final_norm):
    T = x.shape[0]
    moe = ys is not None
    last = final_norm is not None
    tm = _tile(T, ROW_TM // 2 if moe else ROW_TM)
    full = lambda i: (0, 0)
    row = lambda i: (i, 0)
    in_specs = [pl.BlockSpec((tm, D_MODEL), row)]
    args = [x]
    scratch = []
    if moe:
        in_specs += [pl.BlockSpec((tm, D_MODEL), row), pl.BlockSpec((tm, D_MODEL), row),
                     pl.BlockSpec((tm, LANES), row)]
        args += [_sc_gather_rows(ys, dst[:, 0]), _sc_gather_rows(ys, dst[:, 1]), route]
    in_specs += [
        pl.BlockSpec((None, tm, PLE_DIM), lambda i: (layer, i, 0)),
        pl.BlockSpec(wg.shape, full),
        pl.BlockSpec((1, D_MODEL), full),
        pl.BlockSpec(pw.shape, full),
        pl.BlockSpec((1, D_MODEL), full),
    ]
    args += [p, wg, b, pw, pn]
    if last:
        in_specs.append(pl.BlockSpec((1, D_MODEL), full))
        args.append(final_norm)
    return pl.pallas_call(
        functools.partial(_ple_kernel, moe=moe, last=last),
        grid=(T // tm,),
        in_specs=in_specs,
        out_specs=pl.BlockSpec((tm, D_MODEL), row),
        out_shape=jax.ShapeDtypeStruct((T, D_MODEL), F32),
        scratch_shapes=scratch,
        compiler_params=_params(("parallel",)),
        name="ple" + ("_moe" if moe else "") + ("_last" if last else ""),
    )(*args)


def _rope_tables(seq):
    half = ROT_DIM // 2
    pos = jnp.arange(seq, dtype=F32)
    inv = ROPE_THETA ** (-jnp.arange(half, dtype=F32) * 2.0 / ROT_DIM)
    ang = pos[:, None] * inv[None, :]
    cos, sin = jnp.cos(ang), jnp.sin(ang)
    ones = jnp.ones((seq, HEAD_DIM - ROT_DIM), F32)
    zeros = jnp.zeros((seq, HEAD_DIM - ROT_DIM), F32)
    zh = jnp.zeros((seq, half), F32)
    c = jnp.concatenate([cos, cos, ones], axis=1)
    sa = jnp.concatenate([zh, sin, zeros], axis=1)
    sb = jnp.concatenate([-sin, zh, zeros], axis=1)
    rep = lambda t: jnp.concatenate([t, t], axis=1)
    return rep(c), rep(sa), rep(sb)


def _group_layout(route, counts, tm):
    T = route.shape[0]
    counts = counts[0, :N_EXPERTS].astype(jnp.int32)
    n_tile_e = (counts + tm - 1) // tm
    tile_end = jnp.cumsum(n_tile_e)
    start = (tile_end - n_tile_e) * tm
    expert = route[:, 0:TOP_K].astype(jnp.int32)
    rank = route[:, 4:4 + TOP_K].astype(jnp.int32)
    onehot = expert[..., None] == jnp.arange(N_EXPERTS, dtype=jnp.int32)
    dst = jnp.sum(jnp.where(onehot, start, 0), axis=-1) + rank
    n_tiles = TOP_K * T // tm + N_EXPERTS
    tile = jnp.arange(n_tiles, dtype=jnp.int32)
    tile_expert = jnp.minimum(
        jnp.sum((tile[:, None] >= tile_end[None, :]).astype(jnp.int32), axis=1), N_EXPERTS - 1)
    own = tile_expert[:, None] == jnp.arange(N_EXPERTS, dtype=jnp.int32)[None, :]
    left = jnp.sum(jnp.where(own, counts - (tile[:, None] - (tile_end - n_tile_e)) * tm, 0), axis=1)
    n_valid = jnp.where(tile < tile_end[-1], jnp.clip(left, 0, tm), 0).astype(jnp.int32)
    return dst, tile_expert, n_valid, n_tiles * tm


def _trunk(x, p, W):
    B, S, _ = x.shape
    T = B * S
    depth = p.shape[0]
    x = x.reshape(T, D_MODEL)
    p = p.reshape(depth, T, PLE_DIM)
    cos, sa, sb = _rope_tables(S)
    for i in range(depth):
        lam_init = 0.8 - 0.6 * math.exp(-0.3 * i)
        q, k, v, gm = _inproj(x, W["norm_mix"][i], W["w_in"][i], cos, sa, sb, W["gmlp_ln_g"][i],
                              W["gmlp_ln_b"][i], W["gmlp_ws"][i], W["gmlp_bs"][i], S)
        att = _attention(q, k, v, W["lam"][i], W["subln"][i], B, S, lam_init)
        moe = i % 2 == 1
        j = i // 2
        fin = W["final_norm"] if i == depth - 1 else None
        ple_w = (W["ple_gate_w"][i], W["ple_gate_b"][i], W["ple_w"][i], W["ple_norm"][i], fin)
        if not moe:
            x1, h2 = _outproj(att, gm, x, W["w_out"][i], W["norm_ffn"][i], None)
            x2 = _dense_ffn(h2, x1, W["dense_w_gate"], W["dense_w_up"], W["dense_w_down"], j)
            x = _ple(x2, None, None, None, p, i, *ple_w)
        else:
            x1, hp, route, counts = _outproj(att, gm, x, W["w_out"][i], W["norm_ffn"][i],
                                             W["router"][j])
            tm = _tile(T, MOE_TM)
            dst, tile_expert, n_valid, n_rows = _group_layout(route, counts, tm)
            xs = _sc_scatter_rows(hp, dst, n_rows)
            ys = _moe_ffn(xs, tile_expert, n_valid, W["moe_w_gate"], W["moe_w_up"],
                          W["moe_w_down"], j, tm)
            x = _ple(x1, dst, ys, route, p, i, *ple_w)
    return x.reshape(B, S, D_MODEL)


def kernel(x_prompt, x_sample, p_prompt, p_sample, norm_mix, w_in, lambda_q1, lambda_k1, lambda_q2, lambda_k2, subln, gmlp_ln_g, gmlp_ln_b, gmlp_ws, gmlp_bs, w_out, norm_ffn, dense_w_gate, dense_w_up, dense_w_down, router, moe_w_gate, moe_w_up, moe_w_down, ple_w, ple_norm, ple_gate_w, ple_gate_b, final_norm):
    depth = w_in.shape[0]
    row = lambda t: t[:, None, :]
    lam = jnp.stack([lambda_q1, lambda_k1, lambda_q2, lambda_k2], axis=1)
    lam = jnp.pad(lam, ((0, 0), (0, 4), (0, LANES - HEAD_DIM)))
    r_hi = router.astype(BF16)
    r_lo = (router - r_hi.astype(F32)).astype(BF16)
    W = dict(
        norm_mix=row(norm_mix), w_in=w_in.astype(BF16), lam=lam, subln=row(subln),
        gmlp_ln_g=row(gmlp_ln_g), gmlp_ln_b=row(gmlp_ln_b), gmlp_ws=gmlp_ws.astype(BF16),
        gmlp_bs=jnp.broadcast_to(gmlp_bs[..., None], gmlp_bs.shape + (CHUNK,)),
        w_out=w_out.astype(BF16), norm_ffn=row(norm_ffn),
        dense_w_gate=dense_w_gate.astype(BF16), dense_w_up=dense_w_up.astype(BF16),
        dense_w_down=dense_w_down.astype(BF16),
        router=jnp.pad(jnp.concatenate([r_hi, r_lo], axis=-1),
                       ((0, 0), (0, 0), (0, LANES - 2 * N_EXPERTS))),
        moe_w_gate=moe_w_gate.astype(BF16), moe_w_up=moe_w_up.astype(BF16),
        moe_w_down=moe_w_down.astype(BF16),
        ple_w=ple_w.astype(BF16), ple_norm=row(ple_norm), ple_gate_w=ple_gate_w.astype(BF16),
        ple_gate_b=row(ple_gate_b), final_norm=final_norm[None, :],
    )
    return (_trunk(x_prompt, p_prompt, W), _trunk(x_sample, p_sample, W))
```
